```python
import math
import jax, jax.numpy as jnp
from jax import lax
import numpy as np

D_MODEL = 1024
BATCH = 32
SEQ = 2048
DEPTH = 1

GM_HEADS = 8
GM_HEAD_DIM = 128
GM_WIDTH = GM_HEADS * GM_HEAD_DIM
CHUNK = 128
DA_HEADS = 8
DA_QK_DIM = 64
DA_V_DIM = 2 * DA_QK_DIM
DA_WIDTH = DA_HEADS * DA_V_DIM
DA_QK_WIDTH = DA_HEADS * 2 * DA_QK_DIM
ROPE_THETA = 10000.0
Q_BLOCK = 128
EPS = 1e-6

SPLITS = [GM_WIDTH, GM_WIDTH, GM_WIDTH,
          DA_QK_WIDTH, DA_QK_WIDTH, DA_WIDTH, DA_WIDTH,
          D_MODEL, D_MODEL]
D_IN = sum(SPLITS)
SPLIT_POINTS = [int(s) for s in np.cumsum(SPLITS)[:-1]]

kernel_name = "hybrid_gmlp_diffattn_gated_block"


def rmsnorm(x, g):
    xf = x.astype(jnp.float32)
    out = xf * lax.rsqrt(jnp.mean(xf * xf, axis=-1, keepdims=True) + EPS)
    return (out * g.astype(jnp.float32)).astype(x.dtype)


def layernorm(x, g, b):
    xf = x.astype(jnp.float32)
    mu = jnp.mean(xf, axis=-1, keepdims=True)
    var = jnp.mean(jnp.square(xf - mu), axis=-1, keepdims=True)
    out = (xf - mu) * lax.rsqrt(var + EPS) * g.astype(jnp.float32) + b.astype(jnp.float32)
    return out.astype(x.dtype)


def rope(x, cos, sin):
    half = x.shape[-1] // 2
    x1, x2 = x[..., :half], x[..., half:]
    cos = cos.astype(x.dtype)
    sin = sin.astype(x.dtype)
    return jnp.concatenate([x1 * cos - x2 * sin, x2 * cos + x1 * sin], axis=-1)


def gmlp_branch(u, v, ln_g, ln_b, ws, bs):
    B, S, _ = v.shape
    nc = S // CHUNK
    vn = layernorm(v, ln_g, ln_b).reshape(B, nc, CHUNK, GM_HEADS, GM_HEAD_DIM)
    sv = jnp.einsum('hij,bcjhd->bcihd', ws.astype(v.dtype), vn)
    sv = sv + bs.T.astype(v.dtype)[None, None, :, :, None]
    return u * sv.reshape(B, S, GM_WIDTH)


def diff_attention(q, k, v, lq1, lk1, lq2, lk2, subln_g, lam_init, cos, sin):
    B, S, _ = q.shape
    q = q.reshape(B, S, DA_HEADS, 2, DA_QK_DIM)
    k = k.reshape(B, S, DA_HEADS, 2, DA_QK_DIM)
    v = v.reshape(B, S, DA_HEADS, DA_V_DIM)
    scale = DA_QK_DIM ** -0.5
    q1 = rope(q[..., 0, :], cos, sin) * scale
    q2 = rope(q[..., 1, :], cos, sin) * scale
    k1 = rope(k[..., 0, :], cos, sin)
    k2 = rope(k[..., 1, :], cos, sin)
    lam = (jnp.exp(jnp.sum(lq1.astype(jnp.float32) * lk1.astype(jnp.float32)))
           - jnp.exp(jnp.sum(lq2.astype(jnp.float32) * lk2.astype(jnp.float32)))
           + lam_init)
    nb = S // Q_BLOCK
    q1b = q1.reshape(B, nb, Q_BLOCK, DA_HEADS, DA_QK_DIM).transpose(1, 0, 2, 3, 4)
    q2b = q2.reshape(B, nb, Q_BLOCK, DA_HEADS, DA_QK_DIM).transpose(1, 0, 2, 3, 4)

    def block(args):
        a1, a2 = args
        p1 = jax.nn.softmax(jnp.einsum('bqhd,bkhd->bhqk', a1, k1).astype(jnp.float32), axis=-1)
        p2 = jax.nn.softmax(jnp.einsum('bqhd,bkhd->bhqk', a2, k2).astype(jnp.float32), axis=-1)
        attn = (p1 - lam * p2).astype(v.dtype)
        return jnp.einsum('bhqk,bkhe->bqhe', attn, v)

    o = lax.map(block, (q1b, q2b))
    o = o.transpose(1, 0, 2, 3, 4).reshape(B, S, DA_HEADS, DA_V_DIM)
    o = rmsnorm(o, subln_g) * jnp.asarray(1.0 - lam_init, dtype=o.dtype)
    return o.reshape(B, S, DA_WIDTH)


def setup_inputs(seed: int = 0) -> dict:
    key = jax.random.key(seed)
    ks = jax.random.split(key, 16)
    f32 = jnp.float32
    n = lambda k, shape: jax.random.normal(k, shape, dtype=f32)
    return {
        "x": n(ks[0], (BATCH, SEQ, D_MODEL)),
        "ln_pre_g": 1.0 + 0.05 * n(ks[1], (DEPTH, D_MODEL)),
        "w_in": n(ks[2], (DEPTH, D_MODEL, D_IN)) * D_MODEL ** -0.5,
        "gm_ln_g": 1.0 + 0.05 * n(ks[3], (DEPTH, GM_WIDTH)),
        "gm_ln_b": 0.05 * n(ks[4], (DEPTH, GM_WIDTH)),
        "gm_ws": n(ks[5], (DEPTH, GM_HEADS, CHUNK, CHUNK)) * CHUNK ** -0.5,
        "gm_bs": 1.0 + 0.1 * n(ks[6], (DEPTH, GM_HEADS, CHUNK)),
        "lambda_q1": 0.1 * n(ks[7], (DEPTH, DA_QK_DIM)),
        "lambda_k1": 0.1 * n(ks[8], (DEPTH, DA_QK_DIM)),
        "lambda_q2": 0.1 * n(ks[9], (DEPTH, DA_QK_DIM)),
        "lambda_k2": 0.1 * n(ks[10], (DEPTH, DA_QK_DIM)),
        "da_subln_g": 1.0 + 0.05 * n(ks[11], (DEPTH, DA_V_DIM)),
        "w_branch_a": n(ks[12], (DEPTH, GM_WIDTH, D_MODEL)) * GM_WIDTH ** -0.5,
        "w_branch_b": n(ks[13], (DEPTH, DA_WIDTH, D_MODEL)) * DA_WIDTH ** -0.5,
        "w_out": n(ks[14], (DEPTH, D_MODEL, D_MODEL)) * D_MODEL ** -0.5,
        "ln_post_g": 1.0 + 0.05 * n(ks[15], (DEPTH, D_MODEL)),
    }


def reference(x, ln_pre_g, w_in, gm_ln_g, gm_ln_b, gm_ws, gm_bs, lambda_q1, lambda_k1,
              lambda_q2, lambda_k2, da_subln_g, w_branch_a, w_branch_b, w_out, ln_post_g):
    S = x.shape[1]
    pos = jnp.arange(S, dtype=jnp.float32)
    inv_freq = 1.0 / (ROPE_THETA ** (jnp.arange(0, DA_QK_DIM, 2, dtype=jnp.float32) / DA_QK_DIM))
    ang = pos[:, None] * inv_freq[None, :]
    cos = jnp.cos(ang)[:, None, :]
    sin = jnp.sin(ang)[:, None, :]
    for l in range(DEPTH):
        lam_init = 0.8 - 0.6 * math.exp(-0.3 * l)
        h = rmsnorm(x, ln_pre_g[l])
        proj = jnp.einsum('bsd,de->bse', h, w_in[l])
        u, v, za, q, k, vv, zb, ga, gb = jnp.split(proj, SPLIT_POINTS, axis=-1)
        ya = gmlp_branch(u, v, gm_ln_g[l], gm_ln_b[l], gm_ws[l], gm_bs[l]) * jax.nn.silu(za)
        yb = diff_attention(q, k, vv, lambda_q1[l], lambda_k1[l], lambda_q2[l], lambda_k2[l],
                            da_subln_g[l], lam_init, cos, sin) * jax.nn.silu(zb)
        merged = (jax.nn.sigmoid(ga) * jnp.einsum('bse,ed->bsd', ya, w_branch_a[l])
                  + jax.nn.sigmoid(gb) * jnp.einsum('bse,ed->bsd', yb, w_branch_b[l]))
        out = jnp.einsum('bsd,de->bse', merged, w_out[l])
        x = x + rmsnorm(out, ln_post_g[l])
    return x
```

```python
import functools
import math

import jax
import jax.numpy as jnp
from jax import lax
from jax.experimental import pallas as pl
from jax.experimental.pallas import tpu as pltpu

D_MODEL = 1024
GM_HEADS = 8
GM_HEAD_DIM = 128
CHUNK = 128
DA_HEADS = 8
DA_QK_DIM = 64
DA_V_DIM = 128
ROPE_THETA = 10000.0
EPS = 1e-6
N_SPLITS = 9
LAM_INIT = 0.8 - 0.6 * math.exp(-0.3 * 0)

LANES = 128
VMEM_LIMIT_BYTES = 56 * 1024 * 1024

PROJ_ROWS = 512
ATTN_Q_ROWS = 256
OUT_ROWS = 512

F32 = jnp.float32
BF16 = jnp.bfloat16


def _sigmoid(x):
    return 1.0 / (1.0 + jnp.exp(-x))


def _rope_slab(xs, cos, sin_signed, first_half):
    partner = jnp.where(first_half, pltpu.roll(xs, 96, 1), pltpu.roll(xs, 32, 1))
    return xs * cos + partner * sin_signed


def _proj_kernel(x_ref, g_ref, w_ref, lng_ref, lnb_ref, cos_ref, sin_ref,
                 u_o, vn_o, za_o, q_o, k_o, vv_o, zb_o, ga_o, gb_o):
    x = x_ref[...]
    h = x * lax.rsqrt(jnp.mean(x * x, axis=-1, keepdims=True) + EPS)
    h = (h * g_ref[...]).astype(BF16)

    def proj(j):
        return jnp.dot(h, w_ref[:, j * D_MODEL:(j + 1) * D_MODEL],
                       preferred_element_type=F32)

    u_o[...] = proj(0).astype(BF16)

    v = proj(1)
    mu = jnp.mean(v, axis=-1, keepdims=True)
    vc = v - mu
    var = jnp.mean(vc * vc, axis=-1, keepdims=True)
    vn_o[...] = (vc * lax.rsqrt(var + EPS) * lng_ref[...] + lnb_ref[...]).astype(BF16)

    za = proj(2)
    za_o[...] = (za * _sigmoid(za)).astype(BF16)

    cos = cos_ref[...]
    sin_signed = sin_ref[...]
    lane = lax.broadcasted_iota(jnp.int32, cos.shape, 1)
    first_half = (lane & 32) == 0
    q = proj(3)
    for hd in range(DA_HEADS):
        sl = slice(hd * LANES, (hd + 1) * LANES)
        q_o[:, sl] = (_rope_slab(q[:, sl], cos, sin_signed, first_half)
                      * (DA_QK_DIM ** -0.5)).astype(BF16)
    k = proj(4)
    for hd in range(DA_HEADS):
        sl = slice(hd * LANES, (hd + 1) * LANES)
        k_o[:, sl] = _rope_slab(k[:, sl], cos, sin_signed, first_half).astype(BF16)

    vv_o[...] = proj(5).astype(BF16)
    zb = proj(6)
    zb_o[...] = (zb * _sigmoid(zb)).astype(BF16)
    ga_o[...] = _sigmoid(proj(7)).astype(BF16)
    gb_o[...] = _sigmoid(proj(8)).astype(BF16)


def _attn_kernel(lq1_ref, lk1_ref, lq2_ref, lk2_ref, g_ref,
                 q_ref, k_ref, v_ref, zb_ref, o_ref):
    lam = (jnp.exp(jnp.sum(lq1_ref[...] * lk1_ref[...], axis=-1, keepdims=True))
           - jnp.exp(jnp.sum(lq2_ref[...] * lk2_ref[...], axis=-1, keepdims=True))
           + LAM_INIT)
    k = k_ref[0]
    v = v_ref[0]
    g = g_ref[...] * (1.0 - LAM_INIT)
    seq = k.shape[0]
    tq = ATTN_Q_ROWS
    lane = lax.broadcasted_iota(jnp.int32, (tq, LANES), 1)
    is_map1 = lane < DA_QK_DIM

    def body(t, carry):
        r0 = pl.multiple_of(t * tq, tq)
        q = q_ref[0, pl.ds(r0, tq), :]
        zero = jnp.zeros_like(q)
        q12 = jnp.concatenate([jnp.where(is_map1, q, zero),
                               jnp.where(is_map1, zero, q)], axis=0)
        s = lax.dot_general(q12, k, (((1,), (1,)), ((), ())),
                            preferred_element_type=F32)
        s1 = s[:tq]
        s2 = s[tq:]
        e1 = jnp.exp(s1 - jnp.max(s1, axis=-1, keepdims=True))
        e2 = jnp.exp(s2 - jnp.max(s2, axis=-1, keepdims=True))
        r1 = 1.0 / jnp.sum(e1, axis=-1, keepdims=True)
        r2 = lam / jnp.sum(e2, axis=-1, keepdims=True)
        p = (e1 * r1 - e2 * r2).astype(BF16)
        o = jnp.dot(p, v, preferred_element_type=F32)
        o = o * lax.rsqrt(jnp.mean(o * o, axis=-1, keepdims=True) + EPS)
        o = o * g * zb_ref[0, pl.ds(r0, tq), :].astype(F32)
        o_ref[0, pl.ds(r0, tq), :] = o.astype(BF16)
        return carry

    lax.fori_loop(0, seq // tq, body, 0)


def _out_kernel(x_ref, u_ref, vn_ref, za_ref, yb_ref, ga_ref, gb_ref,
                ws_ref, bs_ref, wa_ref, wb_ref, wo_ref, gpost_ref, o_ref):
    rows = x_ref.shape[0]
    parts = []
    for c in range(rows // CHUNK):
        rs = slice(c * CHUNK, (c + 1) * CHUNK)
        heads = []
        for hd in range(GM_HEADS):
            cs = slice(hd * GM_HEAD_DIM, (hd + 1) * GM_HEAD_DIM)
            heads.append(jnp.dot(ws_ref[hd], vn_ref[rs, cs],
                                 preferred_element_type=F32))
        sv = jnp.concatenate(heads, axis=1) + bs_ref[...]
        ya = u_ref[rs, :].astype(F32) * sv * za_ref[rs, :].astype(F32)
        parts.append(ya.astype(BF16))
    ya = jnp.concatenate(parts, axis=0)
    a = jnp.dot(ya, wa_ref[...], preferred_element_type=F32)
    b = jnp.dot(yb_ref[...], wb_ref[...], preferred_element_type=F32)
    merged = (ga_ref[...].astype(F32) * a + gb_ref[...].astype(F32) * b).astype(BF16)
    out = jnp.dot(merged, wo_ref[...], preferred_element_type=F32)
    out = out * lax.rsqrt(jnp.mean(out * out, axis=-1, keepdims=True) + EPS)
    o_ref[...] = x_ref[...] + out * gpost_ref[...]


def _rope_tables(seq):
    pos = jnp.arange(seq, dtype=F32)
    inv_freq = 1.0 / (ROPE_THETA ** (jnp.arange(0, DA_QK_DIM, 2, dtype=F32) / DA_QK_DIM))
    ang = pos[:, None] * inv_freq[None, :]
    cos = jnp.cos(ang)
    sin = jnp.sin(ang)
    cos_t = jnp.tile(cos, (1, LANES // (DA_QK_DIM // 2)))
    sin_t = jnp.tile(jnp.concatenate([-sin, sin], axis=1), (1, LANES // DA_QK_DIM))
    return cos_t, sin_t


def _const_spec(shape):
    return pl.BlockSpec(shape, lambda *_: (0,) * len(shape))


def kernel(x, ln_pre_g, w_in, gm_ln_g, gm_ln_b, gm_ws, gm_bs, lambda_q1, lambda_k1,
           lambda_q2, lambda_k2, da_subln_g, w_branch_a, w_branch_b, w_out, ln_post_g):
    batch, seq, d = x.shape
    assert d == D_MODEL and ln_pre_g.shape[0] == 1
    tokens = batch * seq
    x2 = x.reshape(tokens, d)
    cos_t, sin_t = _rope_tables(seq)
    act = jax.ShapeDtypeStruct((tokens, d), BF16)

    tm = PROJ_ROWS
    seq_tiles = seq // tm
    row_spec = pl.BlockSpec((tm, d), lambda i: (i, 0))
    rope_spec = pl.BlockSpec((tm, LANES), lambda i: (i % seq_tiles, 0))
    u, vn, za, q, k, vv, zb, ga, gb = pl.pallas_call(
        _proj_kernel,
        grid=(tokens // tm,),
        in_specs=[row_spec, _const_spec((1, d)),
                  pl.BlockSpec((d, N_SPLITS * d), lambda i: (0, 0),
                               pipeline_mode=pl.Buffered(1)),
                  _const_spec((1, d)), _const_spec((1, d)), rope_spec, rope_spec],
        out_specs=[row_spec] * N_SPLITS,
        out_shape=[act] * N_SPLITS,
        compiler_params=pltpu.CompilerParams(
            dimension_semantics=("arbitrary",), vmem_limit_bytes=VMEM_LIMIT_BYTES),
        name="in_proj",
    )(x2, ln_pre_g, w_in[0].astype(BF16), gm_ln_g, gm_ln_b, cos_t, sin_t)

    def bsd(t):
        return t.reshape(batch, seq, d)

    head_spec = pl.BlockSpec((1, seq, LANES), lambda b, h: (b, 0, h))
    lam_spec = _const_spec((1, DA_QK_DIM))
    yb = pl.pallas_call(
        _attn_kernel,
        grid=(batch, DA_HEADS),
        in_specs=[lam_spec, lam_spec, lam_spec, lam_spec, _const_spec((1, DA_V_DIM)),
                  head_spec, head_spec, head_spec, head_spec],
        out_specs=head_spec,
        out_shape=jax.ShapeDtypeStruct((batch, seq, d), BF16),
        compiler_params=pltpu.CompilerParams(
            dimension_semantics=("arbitrary", "arbitrary"),
            vmem_limit_bytes=VMEM_LIMIT_BYTES),
        name="diff_attn",
    )(lambda_q1, lambda_k1, lambda_q2, lambda_k2, da_subln_g,
      bsd(q), bsd(k), bsd(vv), bsd(zb))

    to = OUT_ROWS
    orow = pl.BlockSpec((to, d), lambda i: (i, 0))
    bs_full = jnp.repeat(gm_bs[0].T, GM_HEAD_DIM, axis=1)
    out = pl.pallas_call(
        _out_kernel,
        grid=(tokens // to,),
        in_specs=[orow] * 7 + [
            _const_spec((GM_HEADS, CHUNK, CHUNK)), _const_spec((CHUNK, d)),
            _const_spec((d, d)), _const_spec((d, d)), _const_spec((d, d)),
            _const_spec((1, d))],
        out_specs=orow,
        out_shape=jax.ShapeDtypeStruct((tokens, d), F32),
        compiler_params=pltpu.CompilerParams(
            dimension_semantics=("arbitrary",), vmem_limit_bytes=VMEM_LIMIT_BYTES),
        name="gate_out",
    )(x2, u, vn, za, yb.reshape(tokens, d), ga, gb,
      gm_ws[0].astype(BF16), bs_full, w_branch_a[0].astype(BF16),
      w_branch_b[0].astype(BF16), w_out[0].astype(BF16), ln_post_g)
    return out.reshape(batch, seq, d)
```

```python
import math

import jax
import jax.numpy as jnp
from jax import lax
from jax.experimental import pallas as pl
from jax.experimental.pallas import tpu as pltpu

D_MODEL = 1024
GM_HEADS = 8
GM_HEAD_DIM = 128
CHUNK = 128
DA_HEADS = 8
DA_QK_DIM = 64
DA_V_DIM = 128
ROPE_THETA = 10000.0
EPS = 1e-6
N_SPLITS = 9
LAM_INIT = 0.8 - 0.6 * math.exp(-0.3 * 0)
LOG2_E = math.log2(math.e)

LANES = 128
SUBLANES = 8
VMEM_LIMIT_BYTES = 56 * 1024 * 1024

PROJ_ROWS = 512
ATTN_HEADS_PER_STEP = 4
ATTN_Q_ROWS = 256
ATTN_KEY_CHUNK = 256
ATTN_EXP_CHUNK = 64
OUT_ROWS = 512

F32 = jnp.float32
BF16 = jnp.bfloat16


def _sigmoid(x):
    return 1.0 / (1.0 + jnp.exp(-x))


def _rope_slab(xs, cos, sin_signed, first_half):
    partner = jnp.where(first_half, pltpu.roll(xs, 96, 1), pltpu.roll(xs, 32, 1))
    return xs * cos + partner * sin_signed


def _proj_kernel(x_ref, g_ref, w_ref, lng_ref, lnb_ref, cos_ref, sin_ref,
                 u_o, vn_o, za_o, q_o, k_o, vv_o, zb_o, ga_o, gb_o):
    x = x_ref[...]
    h = x * lax.rsqrt(jnp.mean(x * x, axis=-1, keepdims=True) + EPS)
    h = (h * g_ref[...]).astype(BF16)

    def proj(j):
        return jnp.dot(h, w_ref[:, j * D_MODEL:(j + 1) * D_MODEL],
                       preferred_element_type=F32)

    def head_slab(hd):
        return slice(hd * LANES, (hd + 1) * LANES)

    u_o[...] = proj(0).astype(BF16)

    v = proj(1)
    mu = jnp.mean(v, axis=-1, keepdims=True)
    vc = v - mu
    var = jnp.mean(vc * vc, axis=-1, keepdims=True)
    vn_o[...] = (vc * lax.rsqrt(var + EPS) * lng_ref[...] + lnb_ref[...]).astype(BF16)

    za = proj(2)
    za_o[...] = (za * _sigmoid(za)).astype(BF16)

    cos = cos_ref[...]
    sin_signed = sin_ref[...]
    lane = lax.broadcasted_iota(jnp.int32, cos.shape, 1)
    first_half = (lane & 32) == 0
    q_scale = DA_QK_DIM ** -0.5 * LOG2_E
    q = proj(3)
    for hd in range(DA_HEADS):
        q_o[0, hd] = (_rope_slab(q[:, head_slab(hd)], cos, sin_signed, first_half)
                      * q_scale).astype(BF16)
    k = proj(4)
    for hd in range(DA_HEADS):
        k_o[0, hd] = _rope_slab(k[:, head_slab(hd)], cos, sin_signed,
                                first_half).astype(BF16)
    vv = proj(5).astype(BF16)
    for hd in range(DA_HEADS):
        vv_o[0, hd] = vv[:, head_slab(hd)]
    zb = proj(6)
    zb = (zb * _sigmoid(zb)).astype(BF16)
    for hd in range(DA_HEADS):
        zb_o[0, hd] = zb[:, head_slab(hd)]
    ga_o[...] = _sigmoid(proj(7)).astype(BF16)
    gb_o[...] = _sigmoid(proj(8)).astype(BF16)


def _attn_kernel(lq1_ref, lk1_ref, lq2_ref, lk2_ref, g_ref,
                 q_ref, k_ref, v_ref, zb_ref, o_ref, vt_scr, s0_scr, s1_scr,
                 e0_scr, e1_scr):
    lam = (jnp.exp(jnp.sum(lq1_ref[...] * lk1_ref[...], axis=-1, keepdims=True))
           - jnp.exp(jnp.sum(lq2_ref[...] * lk2_ref[...], axis=-1, keepdims=True))
           + LAM_INIT)
    g = g_ref[...] * (1.0 - LAM_INIT)
    heads = q_ref.shape[1]
    seq = q_ref.shape[2]
    tq = ATTN_Q_ROWS
    tiles_per_head = seq // tq
    n_tiles = heads * tiles_per_head
    for hd in range(heads):
        vt_scr[hd] = v_ref[0, hd].T
    lane = lax.broadcasted_iota(jnp.int32, (tq, LANES), 1)
    is_map1 = lane < DA_QK_DIM

    s_slots = (s0_scr, s1_scr)
    e_slots = (e0_scr, e1_scr)

    def locate(j):
        if isinstance(j, int):
            return j // tiles_per_head, (j % tiles_per_head) * tq
        hd = lax.div(j, tiles_per_head)
        r0 = pl.multiple_of(lax.rem(j, tiles_per_head) * tq, tq)
        return hd, r0

    n_key_chunks = seq // ATTN_KEY_CHUNK
    n_exp_chunks = seq // ATTN_EXP_CHUNK

    def masked_queries(j):
        hd, r0 = locate(j)
        q = q_ref[0, hd, pl.ds(r0, tq), :]
        zero = jnp.zeros_like(q)
        return jnp.concatenate([jnp.where(is_map1, q, zero),
                                jnp.where(is_map1, zero, q)], axis=0)

    def score_chunk(hd, q12, slot, c, m_acc):
        ks = pl.ds(c * ATTN_KEY_CHUNK, ATTN_KEY_CHUNK)
        s_c = lax.dot_general(k_ref[0, hd, ks, :], q12, (((1,), (1,)), ((), ())),
                              preferred_element_type=F32)
        s_slots[slot][ks, :] = s_c
        return jnp.maximum(m_acc, jnp.max(s_c.reshape(-1, SUBLANES, 2 * tq), axis=0))

    def exp_chunk(slot, c, m, l_acc):
        ks = pl.ds(c * ATTN_EXP_CHUNK, ATTN_EXP_CHUNK)
        e = jnp.exp2(s_slots[slot][ks, :] - m)
        e_slots[slot][ks, :] = e.astype(BF16)
        return l_acc + jnp.sum(e.reshape(-1, SUBLANES, 2 * tq), axis=0)

    def value_finish(j, ot, l):
        hd, r0 = locate(j)
        r1 = 1.0 / l[:, :tq]
        r2 = lam / l[:, tq:]
        o = (ot[:, :tq] * r1 - ot[:, tq:] * r2).T
        o = o * lax.rsqrt(jnp.mean(o * o, axis=-1, keepdims=True) + EPS)
        o = o * g * zb_ref[0, hd, pl.ds(r0, tq), :].astype(F32)
        o_ref[0, hd, pl.ds(r0, tq), :] = o.astype(BF16)

    def step(j_score, j_exp, j_value, parity, m_prev, l_prev):
        exp_done = 0
        l_acc = jnp.zeros((SUBLANES, 2 * tq), F32)

        def run_exp(upto, l_acc, exp_done):
            if j_exp is not None:
                for c in range(exp_done, upto):
                    l_acc = exp_chunk(1 - parity, c, m_prev, l_acc)
            return l_acc, upto

        head_share = n_exp_chunks // 4 if j_value is not None else 0
        per_chunk = (n_exp_chunks - head_share) // n_key_chunks
        if j_value is not None:
            hd_v, _ = locate(j_value)
            l_acc, exp_done = run_exp(head_share // 2, l_acc, exp_done)
            ot = jnp.dot(vt_scr[hd_v], e_slots[parity][...],
                         preferred_element_type=F32)
            l_acc, exp_done = run_exp(head_share, l_acc, exp_done)
            value_finish(j_value, ot, l_prev)
        m_new = None
        if j_score is not None:
            hd_s, _ = locate(j_score)
            q12 = masked_queries(j_score)
            m_acc = jnp.full((SUBLANES, 2 * tq), -jnp.inf, F32)
            for c in range(n_key_chunks):
                m_acc = score_chunk(hd_s, q12, parity, c, m_acc)
                l_acc, exp_done = run_exp(head_share + (c + 1) * per_chunk,
                                          l_acc, exp_done)
            m_new = jnp.max(m_acc, axis=0, keepdims=True)
        l_acc, exp_done = run_exp(n_exp_chunks, l_acc, exp_done)
        l_new = None
        if j_exp is not None:
            l_new = jnp.sum(l_acc, axis=0, keepdims=True)
        return m_new, l_new

    m0, _ = step(0, None, None, 0, None, None)
    m1, l0 = step(1, 0, None, 1, m0, None)

    def body(j, carry):
        return lax.cond(lax.rem(j, 2) == 0,
                        lambda m, l: step(j, j - 1, j - 2, 0, m, l),
                        lambda m, l: step(j, j - 1, j - 2, 1, m, l), *carry)

    m_last, l_prev = lax.fori_loop(2, n_tiles, body, (m1, l0))
    _, l_last = step(None, n_tiles - 1, n_tiles - 2, n_tiles % 2, m_last, l_prev)
    step(None, None, n_tiles - 1, (n_tiles - 1) % 2, None, l_last)


def _out_kernel(x_ref, u_ref, vn_ref, za_ref, yb_ref, ga_ref, gb_ref,
                ws_ref, bs_ref, wa_ref, wb_ref, wo_ref, gpost_ref, o_ref):
    rows = x_ref.shape[0]
    parts = []
    for c in range(rows // CHUNK):
        rs = slice(c * CHUNK, (c + 1) * CHUNK)
        heads = []
        for hd in range(GM_HEADS):
            cs = slice(hd * GM_HEAD_DIM, (hd + 1) * GM_HEAD_DIM)
            heads.append(jnp.dot(ws_ref[hd], vn_ref[rs, cs],
                                 preferred_element_type=F32))
        sv = jnp.concatenate(heads, axis=1) + bs_ref[...]
        ya = u_ref[rs, :].astype(F32) * sv * za_ref[rs, :].astype(F32)
        parts.append(ya.astype(BF16))
    ya = jnp.concatenate(parts, axis=0)
    yb = jnp.concatenate([yb_ref[0, hd] for hd in range(DA_HEADS)], axis=1)
    a = jnp.dot(ya, wa_ref[...], preferred_element_type=F32)
    b = jnp.dot(yb, wb_ref[...], preferred_element_type=F32)
    merged = (ga_ref[...].astype(F32) * a + gb_ref[...].astype(F32) * b).astype(BF16)
    out = jnp.dot(merged, wo_ref[...], preferred_element_type=F32)
    out = out * lax.rsqrt(jnp.mean(out * out, axis=-1, keepdims=True) + EPS)
    o_ref[...] = x_ref[...] + out * gpost_ref[...]


def _rope_tables(seq):
    pos = jnp.arange(seq, dtype=F32)
    inv_freq = 1.0 / (ROPE_THETA ** (jnp.arange(0, DA_QK_DIM, 2, dtype=F32) / DA_QK_DIM))
    ang = pos[:, None] * inv_freq[None, :]
    cos = jnp.cos(ang)
    sin = jnp.sin(ang)
    cos_t = jnp.tile(cos, (1, LANES // (DA_QK_DIM // 2)))
    sin_t = jnp.tile(jnp.concatenate([-sin, sin], axis=1), (1, LANES // DA_QK_DIM))
    return cos_t, sin_t


def _const_spec(shape):
    return pl.BlockSpec(shape, lambda *_: (0,) * len(shape))


def kernel(x, ln_pre_g, w_in, gm_ln_g, gm_ln_b, gm_ws, gm_bs, lambda_q1, lambda_k1,
           lambda_q2, lambda_k2, da_subln_g, w_branch_a, w_branch_b, w_out, ln_post_g):
    batch, seq, d = x.shape
    assert d == D_MODEL and ln_pre_g.shape[0] == 1
    tokens = batch * seq
    x2 = x.reshape(tokens, d)
    cos_t, sin_t = _rope_tables(seq)
    act = jax.ShapeDtypeStruct((tokens, d), BF16)
    head_act = jax.ShapeDtypeStruct((batch, DA_HEADS, seq, LANES), BF16)

    tm = PROJ_ROWS
    seq_tiles = seq // tm
    row_spec = pl.BlockSpec((tm, d), lambda i: (i, 0))
    head_rows = pl.BlockSpec((1, DA_HEADS, tm, LANES),
                             lambda i: (i // seq_tiles, 0, i % seq_tiles, 0))
    rope_spec = pl.BlockSpec((tm, LANES), lambda i: (i % seq_tiles, 0))
    u, vn, za, q, k, vv, zb, ga, gb = pl.pallas_call(
        _proj_kernel,
        grid=(tokens // tm,),
        in_specs=[row_spec, _const_spec((1, d)),
                  pl.BlockSpec((d, N_SPLITS * d), lambda i: (0, 0),
                               pipeline_mode=pl.Buffered(1)),
                  _const_spec((1, d)), _const_spec((1, d)), rope_spec, rope_spec],
        out_specs=[row_spec, row_spec, row_spec, head_rows, head_rows, head_rows,
                   head_rows, row_spec, row_spec],
        out_shape=[act, act, act, head_act, head_act, head_act, head_act, act, act],
        compiler_params=pltpu.CompilerParams(
            dimension_semantics=("arbitrary",), vmem_limit_bytes=VMEM_LIMIT_BYTES),
        name="in_proj",
    )(x2, ln_pre_g, w_in[0].astype(BF16), gm_ln_g, gm_ln_b, cos_t, sin_t)

    hb = ATTN_HEADS_PER_STEP
    head_spec = pl.BlockSpec((1, hb, seq, LANES), lambda b, h: (b, h, 0, 0))
    lam_spec = _const_spec((1, DA_QK_DIM))
    yb = pl.pallas_call(
        _attn_kernel,
        grid=(batch, DA_HEADS // hb),
        in_specs=[lam_spec, lam_spec, lam_spec, lam_spec, _const_spec((1, DA_V_DIM)),
                  head_spec, head_spec, head_spec, head_spec],
        out_specs=head_spec,
        out_shape=head_act,
        scratch_shapes=[
            pltpu.VMEM((hb, DA_V_DIM, seq), BF16),
            pltpu.VMEM((seq, 2 * ATTN_Q_ROWS), F32),
            pltpu.VMEM((seq, 2 * ATTN_Q_ROWS), F32),
            pltpu.VMEM((seq, 2 * ATTN_Q_ROWS), BF16),
            pltpu.VMEM((seq, 2 * ATTN_Q_ROWS), BF16)],
        compiler_params=pltpu.CompilerParams(
            dimension_semantics=("arbitrary", "arbitrary"),
            vmem_limit_bytes=VMEM_LIMIT_BYTES),
        name="diff_attn",
    )(lambda_q1, lambda_k1, lambda_q2, lambda_k2, da_subln_g, q, k, vv, zb)

    to = OUT_ROWS
    out_tiles = seq // to
    orow = pl.BlockSpec((to, d), lambda i: (i, 0))
    yb_rows = pl.BlockSpec((1, DA_HEADS, to, LANES),
                           lambda i: (i // out_tiles, 0, i % out_tiles, 0))
    bs_full = jnp.repeat(gm_bs[0].T, GM_HEAD_DIM, axis=1)
    out = pl.pallas_call(
        _out_kernel,
        grid=(tokens // to,),
        in_specs=[orow, orow, orow, orow, yb_rows, orow, orow,
                  _const_spec((GM_HEADS, CHUNK, CHUNK)), _const_spec((CHUNK, d)),
                  _const_spec((d, d)), _const_spec((d, d)), _const_spec((d, d)),
                  _const_spec((1, d))],
        out_specs=orow,
        out_shape=jax.ShapeDtypeStruct((tokens, d), F32),
        compiler_params=pltpu.CompilerParams(
            dimension_semantics=("arbitrary",), vmem_limit_bytes=VMEM_LIMIT_BYTES),
        name="gate_out",
    )(x2, u, vn, za, yb, ga, gb,
      gm_ws[0].astype(BF16), bs_full, w_branch_a[0].astype(BF16),
      w_branch_b[0].astype(BF16), w_out[0].astype(BF16), ln_post_g)
    return out.reshape(batch, seq, d)
```

```python
import math

import jax
import jax.numpy as jnp
from jax import lax
from jax.experimental import pallas as pl
from jax.experimental.pallas import tpu as pltpu

D_MODEL = 1024
GM_HEADS = 8
GM_HEAD_DIM = 128
CHUNK = 128
DA_HEADS = 8
DA_QK_DIM = 64
DA_V_DIM = 128
ROPE_THETA = 10000.0
EPS = 1e-6
N_SPLITS = 9
LAM_INIT = 0.8 - 0.6 * math.exp(-0.3 * 0)
LOG2_E = math.log2(math.e)

LANES = 128
SUBLANES = 8
VMEM_LIMIT_BYTES = 56 * 1024 * 1024

PROJ_ROWS = 512
LN_ROWS = 128
ATTN_HEADS_PER_STEP = 8
ATTN_Q_ROWS = 256
ATTN_KEY_CHUNK = 256
ATTN_EXP_CHUNK = 64
ATTN_VALUE_SPLITS = 4
OUT_ROWS = 512

F32 = jnp.float32
BF16 = jnp.bfloat16


def _sigmoid(x):
    return 0.5 * jnp.tanh(0.5 * x) + 0.5


def _silu(x):
    h = 0.5 * x
    return h + h * jnp.tanh(h)


def _rope_slab(xs, cos, sin_signed, first_half):
    partner = jnp.where(first_half, pltpu.roll(xs, 96, 1), pltpu.roll(xs, 32, 1))
    return xs * cos + partner * sin_signed


def _proj_kernel(x_ref, g_ref, w_ref, lng_ref, lnb_ref, cos_ref, sin_ref,
                 u_o, vn_o, za_o, q_o, k_o, vv_o, zb_o, ga_o, gb_o):
    x = x_ref[...]
    h = x * lax.rsqrt(jnp.mean(x * x, axis=-1, keepdims=True) + EPS)
    h = (h * g_ref[...]).astype(BF16)

    def proj(j):
        return jnp.dot(h, w_ref[:, j * D_MODEL:(j + 1) * D_MODEL],
                       preferred_element_type=F32)

    def head_slab(hd):
        return slice(hd * LANES, (hd + 1) * LANES)

    za_o[...] = _silu(proj(2)).astype(BF16)
    ga_o[...] = _sigmoid(proj(7)).astype(BF16)

    v = proj(1)
    for r in range(0, v.shape[0], LN_ROWS):
        vr = v[r:r + LN_ROWS]
        mu = jnp.mean(vr, axis=-1, keepdims=True)
        vc = vr - mu
        var = jnp.mean(vc * vc, axis=-1, keepdims=True)
        vn_o[r:r + LN_ROWS, :] = (vc * lax.rsqrt(var + EPS) * lng_ref[...]
                                  + lnb_ref[...]).astype(BF16)

    gb_o[...] = _sigmoid(proj(8)).astype(BF16)
    zb = _silu(proj(6)).astype(BF16)
    for hd in range(DA_HEADS):
        zb_o[0, hd] = zb[:, head_slab(hd)]
    vv = proj(5)
    for hd in range(DA_HEADS):
        vv_o[0, hd] = vv[:, head_slab(hd)].T.astype(BF16)

    cos = cos_ref[...]
    sin_signed = sin_ref[...]
    lane = lax.broadcasted_iota(jnp.int32, cos.shape, 1)
    first_half = (lane & 32) == 0
    q_scale = DA_QK_DIM ** -0.5 * LOG2_E
    q = proj(3)
    for hd in range(DA_HEADS):
        q_o[0, hd] = (_rope_slab(q[:, head_slab(hd)], cos, sin_signed, first_half)
                      * q_scale).astype(BF16)
    k = proj(4)
    for hd in range(DA_HEADS):
        k_o[0, hd] = _rope_slab(k[:, head_slab(hd)], cos, sin_signed,
                                first_half).astype(BF16)
    u_o[...] = proj(0).astype(BF16)


def _attn_kernel(lq1_ref, lk1_ref, lq2_ref, lk2_ref, g_ref,
                 q_ref, k_ref, vt_ref, zb_ref, o_ref, s0_scr, s1_scr,
                 e0_scr, e1_scr):
    lam = (jnp.exp(jnp.sum(lq1_ref[...] * lk1_ref[...], axis=-1, keepdims=True))
           - jnp.exp(jnp.sum(lq2_ref[...] * lk2_ref[...], axis=-1, keepdims=True))
           + LAM_INIT)
    g = g_ref[...] * (1.0 - LAM_INIT)
    heads = q_ref.shape[1]
    seq = q_ref.shape[2]
    tq = ATTN_Q_ROWS
    tiles_per_head = seq // tq
    n_tiles = heads * tiles_per_head
    lane = lax.broadcasted_iota(jnp.int32, (tq, LANES), 1)
    is_map1 = lane < DA_QK_DIM

    s_slots = (s0_scr, s1_scr)
    e_slots = (e0_scr, e1_scr)

    def locate(j):
        if isinstance(j, int):
            return j // tiles_per_head, (j % tiles_per_head) * tq
        hd = lax.div(j, tiles_per_head)
        r0 = pl.multiple_of(lax.rem(j, tiles_per_head) * tq, tq)
        return hd, r0

    n_key_chunks = seq // ATTN_KEY_CHUNK
    n_exp_chunks = seq // ATTN_EXP_CHUNK

    def masked_queries(j):
        hd, r0 = locate(j)
        q = q_ref[0, hd, pl.ds(r0, tq), :]
        zero = jnp.zeros_like(q)
        return jnp.concatenate([jnp.where(is_map1, q, zero),
                                jnp.where(is_map1, zero, q)], axis=0)

    def score_chunk(hd, q12, slot, c, m_acc):
        ks = pl.ds(c * ATTN_KEY_CHUNK, ATTN_KEY_CHUNK)
        s_c = lax.dot_general(k_ref[0, hd, ks, :], q12, (((1,), (1,)), ((), ())),
                              preferred_element_type=F32)
        s_slots[slot][ks, :] = s_c
        return jnp.maximum(m_acc, jnp.max(s_c.reshape(-1, SUBLANES, 2 * tq), axis=0))

    def exp_chunk(slot, c, m, l_acc):
        ks = pl.ds(c * ATTN_EXP_CHUNK, ATTN_EXP_CHUNK)
        e = jnp.exp2(s_slots[slot][ks, :] - m)
        e_slots[slot][ks, :] = e.astype(BF16)
        return l_acc + jnp.sum(e.reshape(-1, SUBLANES, 2 * tq), axis=0)

    def value_finish(j, ot, l):
        hd, r0 = locate(j)
        r1 = 1.0 / l[:, :tq]
        r2 = lam / l[:, tq:]
        o = (ot[:, :tq] * r1 - ot[:, tq:] * r2).T
        o = o * lax.rsqrt(jnp.mean(o * o, axis=-1, keepdims=True) + EPS)
        o = o * g * zb_ref[0, hd, pl.ds(r0, tq), :].astype(F32)
        o_ref[0, hd, pl.ds(r0, tq), :] = o.astype(BF16)

    def step(j_score, j_exp, j_value, parity, m_prev, l_prev):
        exp_done = 0
        l_acc = jnp.zeros((SUBLANES, 2 * tq), F32)

        def run_exp(upto, l_acc, exp_done):
            if j_exp is not None:
                for c in range(exp_done, upto):
                    l_acc = exp_chunk(1 - parity, c, m_prev, l_acc)
            return l_acc, upto

        head_share = n_exp_chunks // 4 if j_value is not None else 0
        per_chunk = (n_exp_chunks - head_share) // n_key_chunks
        if j_value is not None:
            hd_v, _ = locate(j_value)
            kv = seq // ATTN_VALUE_SPLITS
            ot = None
            for part in range(ATTN_VALUE_SPLITS):
                ks = pl.ds(part * kv, kv)
                l_acc, exp_done = run_exp(
                    (part + 1) * head_share // ATTN_VALUE_SPLITS, l_acc, exp_done)
                ot_part = jnp.dot(vt_ref[0, hd_v, :, ks], e_slots[parity][ks, :],
                                  preferred_element_type=F32)
                ot = ot_part if ot is None else ot + ot_part
            value_finish(j_value, ot, l_prev)
        m_new = None
        if j_score is not None:
            hd_s, _ = locate(j_score)
            q12 = masked_queries(j_score)
            m_acc = jnp.full((SUBLANES, 2 * tq), -jnp.inf, F32)
            for c in range(n_key_chunks):
                m_acc = score_chunk(hd_s, q12, parity, c, m_acc)
                l_acc, exp_done = run_exp(head_share + (c + 1) * per_chunk,
                                          l_acc, exp_done)
            m_new = jnp.max(m_acc, axis=0, keepdims=True)
        l_acc, exp_done = run_exp(n_exp_chunks, l_acc, exp_done)
        l_new = None
        if j_exp is not None:
            l_new = jnp.sum(l_acc, axis=0, keepdims=True)
        return m_new, l_new

    m0, _ = step(0, None, None, 0, None, None)
    m1, l0 = step(1, 0, None, 1, m0, None)

    def body(j, carry):
        return lax.cond(lax.rem(j, 2) == 0,
                        lambda m, l: step(j, j - 1, j - 2, 0, m, l),
                        lambda m, l: step(j, j - 1, j - 2, 1, m, l), *carry)

    m_last, l_prev = lax.fori_loop(2, n_tiles, body, (m1, l0))
    _, l_last = step(None, n_tiles - 1, n_tiles - 2, n_tiles % 2, m_last, l_prev)
    step(None, None, n_tiles - 1, (n_tiles - 1) % 2, None, l_last)


def _out_kernel(x_ref, u_ref, vn_ref, za_ref, yb_ref, ga_ref, gb_ref,
                ws_ref, bs_ref, wa_ref, wb_ref, wo_ref, gpost_ref, o_ref):
    rows = x_ref.shape[0]
    parts = []
    for c in range(rows // CHUNK):
        rs = slice(c * CHUNK, (c + 1) * CHUNK)
        heads = []
        for hd in range(GM_HEADS):
            cs = slice(hd * GM_HEAD_DIM, (hd + 1) * GM_HEAD_DIM)
            heads.append(jnp.dot(ws_ref[hd], vn_ref[rs, cs],
                                 preferred_element_type=F32))
        sv = jnp.concatenate(heads, axis=1) + bs_ref[...]
        ya = u_ref[rs, :].astype(F32) * sv * za_ref[rs, :].astype(F32)
        parts.append(ya.astype(BF16))
    ya = jnp.concatenate(parts, axis=0)
    yb = jnp.concatenate([yb_ref[0, hd] for hd in range(DA_HEADS)], axis=1)
    a = jnp.dot(ya, wa_ref[...], preferred_element_type=F32)
    b = jnp.dot(yb, wb_ref[...], preferred_element_type=F32)
    merged = (ga_ref[...].astype(F32) * a + gb_ref[...].astype(F32) * b).astype(BF16)
    out = jnp.dot(merged, wo_ref[...], preferred_element_type=F32)
    out = out * lax.rsqrt(jnp.mean(out * out, axis=-1, keepdims=True) + EPS)
    o_ref[...] = x_ref[...] + out * gpost_ref[...]


def _rope_tables(seq):
    pos = jnp.arange(seq, dtype=F32)
    inv_freq = 1.0 / (ROPE_THETA ** (jnp.arange(0, DA_QK_DIM, 2, dtype=F32) / DA_QK_DIM))
    ang = pos[:, None] * inv_freq[None, :]
    cos = jnp.cos(ang)
    sin = jnp.sin(ang)
    cos_t = jnp.tile(cos, (1, LANES // (DA_QK_DIM // 2)))
    sin_t = jnp.tile(jnp.concatenate([-sin, sin], axis=1), (1, LANES // DA_QK_DIM))
    return cos_t, sin_t


def _const_spec(shape):
    return pl.BlockSpec(shape, lambda *_: (0,) * len(shape))


def kernel(x, ln_pre_g, w_in, gm_ln_g, gm_ln_b, gm_ws, gm_bs, lambda_q1, lambda_k1,
           lambda_q2, lambda_k2, da_subln_g, w_branch_a, w_branch_b, w_out, ln_post_g):
    batch, seq, d = x.shape
    assert d == D_MODEL and ln_pre_g.shape[0] == 1
    tokens = batch * seq
    x2 = x.reshape(tokens, d)
    cos_t, sin_t = _rope_tables(seq)
    act = jax.ShapeDtypeStruct((tokens, d), BF16)
    head_act = jax.ShapeDtypeStruct((batch, DA_HEADS, seq, LANES), BF16)
    head_act_t = jax.ShapeDtypeStruct((batch, DA_HEADS, DA_V_DIM, seq), BF16)

    tm = PROJ_ROWS
    seq_tiles = seq // tm
    row_spec = pl.BlockSpec((tm, d), lambda i: (i, 0))
    head_rows = pl.BlockSpec((1, DA_HEADS, tm, LANES),
                             lambda i: (i // seq_tiles, 0, i % seq_tiles, 0))
    head_cols = pl.BlockSpec((1, DA_HEADS, DA_V_DIM, tm),
                             lambda i: (i // seq_tiles, 0, 0, i % seq_tiles))
    rope_spec = pl.BlockSpec((tm, LANES), lambda i: (i % seq_tiles, 0))
    u, vn, za, q, k, vv, zb, ga, gb = pl.pallas_call(
        _proj_kernel,
        grid=(tokens // tm,),
        in_specs=[row_spec, _const_spec((1, d)),
                  pl.BlockSpec((d, N_SPLITS * d), lambda i: (0, 0),
                               pipeline_mode=pl.Buffered(1)),
                  _const_spec((1, d)), _const_spec((1, d)), rope_spec, rope_spec],
        out_specs=[row_spec, row_spec, row_spec, head_rows, head_rows, head_cols,
                   head_rows, row_spec, row_spec],
        out_shape=[act, act, act, head_act, head_act, head_act_t, head_act, act, act],
        compiler_params=pltpu.CompilerParams(
            dimension_semantics=("arbitrary",), vmem_limit_bytes=VMEM_LIMIT_BYTES),
        name="in_proj",
    )(x2, ln_pre_g, w_in[0].astype(BF16), gm_ln_g, gm_ln_b, cos_t, sin_t)

    hb = ATTN_HEADS_PER_STEP
    head_spec = pl.BlockSpec((1, hb, seq, LANES), lambda b, h: (b, h, 0, 0))
    head_spec_t = pl.BlockSpec((1, hb, DA_V_DIM, seq), lambda b, h: (b, h, 0, 0))
    lam_spec = _const_spec((1, DA_QK_DIM))
    yb = pl.pallas_call(
        _attn_kernel,
        grid=(batch, DA_HEADS // hb),
        in_specs=[lam_spec, lam_spec, lam_spec, lam_spec, _const_spec((1, DA_V_DIM)),
                  head_spec, head_spec, head_spec_t, head_spec],
        out_specs=head_spec,
        out_shape=head_act,
        scratch_shapes=[
            pltpu.VMEM((seq, 2 * ATTN_Q_ROWS), F32),
            pltpu.VMEM((seq, 2 * ATTN_Q_ROWS), F32),
            pltpu.VMEM((seq, 2 * ATTN_Q_ROWS), BF16),
            pltpu.VMEM((seq, 2 * ATTN_Q_ROWS), BF16)],
        compiler_params=pltpu.CompilerParams(
            dimension_semantics=("arbitrary", "arbitrary"),
            vmem_limit_bytes=VMEM_LIMIT_BYTES),
        name="diff_attn",
    )(lambda_q1, lambda_k1, lambda_q2, lambda_k2, da_subln_g, q, k, vv, zb)

    to = OUT_ROWS
    out_tiles = seq // to
    orow = pl.BlockSpec((to, d), lambda i: (i, 0))
    yb_rows = pl.BlockSpec((1, DA_HEADS, to, LANES),
                           lambda i: (i // out_tiles, 0, i % out_tiles, 0))
    bs_full = jnp.repeat(gm_bs[0].T, GM_HEAD_DIM, axis=1)
    out = pl.pallas_call(
        _out_kernel,
        grid=(tokens // to,),
        in_specs=[orow, orow, orow, orow, yb_rows, orow, orow,
                  _const_spec((GM_HEADS, CHUNK, CHUNK)), _const_spec((CHUNK, d)),
                  _const_spec((d, d)), _const_spec((d, d)), _const_spec((d, d)),
                  _const_spec((1, d))],
        out_specs=orow,
        out_shape=jax.ShapeDtypeStruct((tokens, d), F32),
        compiler_params=pltpu.CompilerParams(
            dimension_semantics=("arbitrary",), vmem_limit_bytes=VMEM_LIMIT_BYTES),
        name="gate_out",
    )(x2, u, vn, za, yb, ga, gb,
      gm_ws[0].astype(BF16), bs_full, w_branch_a[0].astype(BF16),
      w_branch_b[0].astype(BF16), w_out[0].astype(BF16), ln_post_g)
    return out.reshape(batch, seq, d)
```

```python
import math

import jax
import jax.numpy as jnp
from jax import lax
from jax.experimental import pallas as pl
from jax.experimental.pallas import tpu as pltpu

D_MODEL = 1024
GM_HEADS = 8
GM_HEAD_DIM = 128
CHUNK = 128
DA_HEADS = 8
DA_QK_DIM = 64
DA_V_DIM = 128
ROPE_THETA = 10000.0
EPS = 1e-6
N_SPLITS = 9
LAM_INIT = 0.8 - 0.6 * math.exp(-0.3 * 0)
LOG2_E = math.log2(math.e)

LANES = 128
SUBLANES = 8
VMEM_LIMIT_BYTES = 56 * 1024 * 1024

PROJ_ROWS = 512
LN_ROWS = 128
ATTN_HEADS_PER_STEP = 4
ATTN_Q_ROWS = 256
ATTN_KEY_CHUNK = 256
ATTN_EXP_CHUNK = 64
OUT_ROWS = 512

F32 = jnp.float32
BF16 = jnp.bfloat16


def _sigmoid(x):
    return 0.5 * jnp.tanh(0.5 * x) + 0.5


def _silu(x):
    h = 0.5 * x
    return h + h * jnp.tanh(h)


def _rope_slab(xs, cos, sin_signed, first_half):
    partner = jnp.where(first_half, pltpu.roll(xs, 96, 1), pltpu.roll(xs, 32, 1))
    return xs * cos + partner * sin_signed


def _proj_kernel(x_ref, g_ref, w_ref, lng_ref, lnb_ref, cos_ref, sin_ref,
                 u_o, vn_o, za_o, q_o, k_o, vv_o, zb_o, ga_o, gb_o):
    def normed(rs):
        x = x_ref[rs, :]
        hx = x * lax.rsqrt(jnp.mean(x * x, axis=-1, keepdims=True) + EPS)
        return (hx * g_ref[...]).astype(BF16)

    def weight(j):
        return w_ref[:, j * D_MODEL:(j + 1) * D_MODEL]

    half = x_ref.shape[0] // 2
    h_halves = []
    for rs in (slice(0, half), slice(half, 2 * half)):
        h_half = normed(rs)
        h_halves.append(h_half)
        za_o[rs, :] = _silu(jnp.dot(h_half, weight(2),
                                    preferred_element_type=F32)).astype(BF16)
    h = jnp.concatenate(h_halves, axis=0)

    def proj(j):
        return jnp.dot(h, weight(j), preferred_element_type=F32)

    def head_slab(hd):
        return slice(hd * LANES, (hd + 1) * LANES)

    ga_o[...] = _sigmoid(proj(7)).astype(BF16)

    v = proj(1)
    for r in range(0, v.shape[0], LN_ROWS):
        vr = v[r:r + LN_ROWS]
        mu = jnp.mean(vr, axis=-1, keepdims=True)
        vc = vr - mu
        var = jnp.mean(vc * vc, axis=-1, keepdims=True)
        vn_o[r:r + LN_ROWS, :] = (vc * lax.rsqrt(var + EPS) * lng_ref[...]
                                  + lnb_ref[...]).astype(BF16)

    gb_o[...] = _sigmoid(proj(8)).astype(BF16)
    zb = _silu(proj(6)).astype(BF16)
    for hd in range(DA_HEADS):
        zb_o[0, hd] = zb[:, head_slab(hd)]
    vv = proj(5).astype(BF16)
    for hd in range(DA_HEADS):
        vv_o[0, hd] = vv[:, head_slab(hd)]

    cos = cos_ref[...]
    sin_signed = sin_ref[...]
    lane = lax.broadcasted_iota(jnp.int32, cos.shape, 1)
    first_half = (lane & 32) == 0
    q_scale = DA_QK_DIM ** -0.5 * LOG2_E
    q = proj(3)
    for hd in range(DA_HEADS):
        q_o[0, hd] = (_rope_slab(q[:, head_slab(hd)], cos, sin_signed, first_half)
                      * q_scale).astype(BF16)
    k = proj(4)
    for hd in range(DA_HEADS):
        k_o[0, hd] = _rope_slab(k[:, head_slab(hd)], cos, sin_signed,
                                first_half).astype(BF16)
    u_o[...] = proj(0).astype(BF16)


def _attn_kernel(lq1_ref, lk1_ref, lq2_ref, lk2_ref, g_ref,
                 q_ref, k_ref, v_ref, zb_ref, o_ref, vt_scr, s0_scr, s1_scr,
                 e0_scr, e1_scr):
    lam = (jnp.exp(jnp.sum(lq1_ref[...] * lk1_ref[...], axis=-1, keepdims=True))
           - jnp.exp(jnp.sum(lq2_ref[...] * lk2_ref[...], axis=-1, keepdims=True))
           + LAM_INIT)
    g = g_ref[...] * (1.0 - LAM_INIT)
    heads = q_ref.shape[1]
    seq = q_ref.shape[2]
    tq = ATTN_Q_ROWS
    tiles_per_head = seq // tq
    n_tiles = heads * tiles_per_head
    for hd in range(heads):
        vt_scr[hd] = v_ref[0, hd].T
    lane = lax.broadcasted_iota(jnp.int32, (tq, LANES), 1)
    is_map1 = lane < DA_QK_DIM

    s_slots = (s0_scr, s1_scr)
    e_slots = (e0_scr, e1_scr)

    def locate(j):
        if isinstance(j, int):
            return j // tiles_per_head, (j % tiles_per_head) * tq
        hd = lax.div(j, tiles_per_head)
        r0 = pl.multiple_of(lax.rem(j, tiles_per_head) * tq, tq)
        return hd, r0

    n_key_chunks = seq // ATTN_KEY_CHUNK
    n_exp_chunks = seq // ATTN_EXP_CHUNK

    def masked_queries(j):
        hd, r0 = locate(j)
        q = q_ref[0, hd, pl.ds(r0, tq), :]
        zero = jnp.zeros_like(q)
        return jnp.concatenate([jnp.where(is_map1, q, zero),
                                jnp.where(is_map1, zero, q)], axis=0)

    def score_chunk(hd, q12, slot, c, m_acc):
        ks = pl.ds(c * ATTN_KEY_CHUNK, ATTN_KEY_CHUNK)
        s_c = lax.dot_general(k_ref[0, hd, ks, :], q12, (((1,), (1,)), ((), ())),
                              preferred_element_type=F32)
        s_slots[slot][ks, :] = s_c
        return jnp.maximum(m_acc, jnp.max(s_c.reshape(-1, SUBLANES, 2 * tq), axis=0))

    def exp_chunk(slot, c, m, l_acc):
        ks = pl.ds(c * ATTN_EXP_CHUNK, ATTN_EXP_CHUNK)
        e = jnp.exp2(s_slots[slot][ks, :] - m)
        e_slots[slot][ks, :] = e.astype(BF16)
        return l_acc + jnp.sum(e.reshape(-1, SUBLANES, 2 * tq), axis=0)

    def value_finish(j, ot, l):
        hd, r0 = locate(j)
        r1 = 1.0 / l[:, :tq]
        r2 = lam / l[:, tq:]
        o = (ot[:, :tq] * r1 - ot[:, tq:] * r2).T
        o = o * lax.rsqrt(jnp.mean(o * o, axis=-1, keepdims=True) + EPS)
        o = o * g * zb_ref[0, hd, pl.ds(r0, tq), :].astype(F32)
        o_ref[0, hd, pl.ds(r0, tq), :] = o.astype(BF16)

    def step(j_score, j_exp, j_value, parity, m_prev, l_prev):
        exp_done = 0
        l_acc = jnp.zeros((SUBLANES, 2 * tq), F32)

        def run_exp(upto, l_acc, exp_done):
            if j_exp is not None:
                for c in range(exp_done, upto):
                    l_acc = exp_chunk(1 - parity, c, m_prev, l_acc)
            return l_acc, upto

        head_share = n_exp_chunks // 4 if j_value is not None else 0
        per_chunk = (n_exp_chunks - head_share) // n_key_chunks
        if j_value is not None:
            hd_v, _ = locate(j_value)
            l_acc, exp_done = run_exp(head_share // 2, l_acc, exp_done)
            ot = jnp.dot(vt_scr[hd_v], e_slots[parity][...],
                         preferred_element_type=F32)
            l_acc, exp_done = run_exp(head_share, l_acc, exp_done)
            value_finish(j_value, ot, l_prev)
        m_new = None
        if j_score is not None:
            hd_s, _ = locate(j_score)
            q12 = masked_queries(j_score)
            m_acc = jnp.full((SUBLANES, 2 * tq), -jnp.inf, F32)
            for c in range(n_key_chunks):
                m_acc = score_chunk(hd_s, q12, parity, c, m_acc)
                l_acc, exp_done = run_exp(head_share + (c + 1) * per_chunk,
                                          l_acc, exp_done)
            m_new = jnp.max(m_acc, axis=0, keepdims=True)
        l_acc, exp_done = run_exp(n_exp_chunks, l_acc, exp_done)
        l_new = None
        if j_exp is not None:
            l_new = jnp.sum(l_acc, axis=0, keepdims=True)
        return m_new, l_new

    m0, _ = step(0, None, None, 0, None, None)
    m1, l0 = step(1, 0, None, 1, m0, None)

    def body(j, carry):
        return lax.cond(lax.rem(j, 2) == 0,
                        lambda m, l: step(j, j - 1, j - 2, 0, m, l),
                        lambda m, l: step(j, j - 1, j - 2, 1, m, l), *carry)

    m_last, l_prev = lax.fori_loop(2, n_tiles, body, (m1, l0))
    _, l_last = step(None, n_tiles - 1, n_tiles - 2, n_tiles % 2, m_last, l_prev)
    step(None, None, n_tiles - 1, (n_tiles - 1) % 2, None, l_last)


def _out_kernel(x_ref, u_ref, vn_ref, za_ref, yb_ref, ga_ref, gb_ref,
                ws_ref, bs_ref, wa_ref, wb_ref, wo_ref, gpost_ref, o_ref):
    rows = x_ref.shape[0]
    parts = []
    for c in range(rows // CHUNK):
        rs = slice(c * CHUNK, (c + 1) * CHUNK)
        heads = []
        for hd in range(GM_HEADS):
            cs = slice(hd * GM_HEAD_DIM, (hd + 1) * GM_HEAD_DIM)
            heads.append(jnp.dot(ws_ref[hd], vn_ref[rs, cs],
                                 preferred_element_type=F32))
        sv = jnp.concatenate(heads, axis=1) + bs_ref[...]
        ya = u_ref[rs, :].astype(F32) * sv * za_ref[rs, :].astype(F32)
        parts.append(ya.astype(BF16))
    ya = jnp.concatenate(parts, axis=0)
    yb = jnp.concatenate([yb_ref[0, hd] for hd in range(DA_HEADS)], axis=1)
    a = jnp.dot(ya, wa_ref[...], preferred_element_type=F32)
    b = jnp.dot(yb, wb_ref[...], preferred_element_type=F32)
    merged = (ga_ref[...].astype(F32) * a + gb_ref[...].astype(F32) * b).astype(BF16)
    out = jnp.dot(merged, wo_ref[...], preferred_element_type=F32)
    out = out * lax.rsqrt(jnp.mean(out * out, axis=-1, keepdims=True) + EPS)
    o_ref[...] = x_ref[...] + out * gpost_ref[...]


def _rope_tables(seq):
    pos = jnp.arange(seq, dtype=F32)
    inv_freq = 1.0 / (ROPE_THETA ** (jnp.arange(0, DA_QK_DIM, 2, dtype=F32) / DA_QK_DIM))
    ang = pos[:, None] * inv_freq[None, :]
    cos = jnp.cos(ang)
    sin = jnp.sin(ang)
    cos_t = jnp.tile(cos, (1, LANES // (DA_QK_DIM // 2)))
    sin_t = jnp.tile(jnp.concatenate([-sin, sin], axis=1), (1, LANES // DA_QK_DIM))
    return cos_t, sin_t


def _const_spec(shape):
    return pl.BlockSpec(shape, lambda *_: (0,) * len(shape))


def kernel(x, ln_pre_g, w_in, gm_ln_g, gm_ln_b, gm_ws, gm_bs, lambda_q1, lambda_k1,
           lambda_q2, lambda_k2, da_subln_g, w_branch_a, w_branch_b, w_out, ln_post_g):
    batch, seq, d = x.shape
    assert d == D_MODEL and ln_pre_g.shape[0] == 1
    tokens = batch * seq
    x2 = x.reshape(tokens, d)
    cos_t, sin_t = _rope_tables(seq)
    act = jax.ShapeDtypeStruct((tokens, d), BF16)
    head_act = jax.ShapeDtypeStruct((batch, DA_HEADS, seq, LANES), BF16)

    tm = PROJ_ROWS
    seq_tiles = seq // tm
    row_spec = pl.BlockSpec((tm, d), lambda i: (i, 0))
    head_rows = pl.BlockSpec((1, DA_HEADS, tm, LANES),
                             lambda i: (i // seq_tiles, 0, i % seq_tiles, 0))
    rope_spec = pl.BlockSpec((tm, LANES), lambda i: (i % seq_tiles, 0))
    u, vn, za, q, k, vv, zb, ga, gb = pl.pallas_call(
        _proj_kernel,
        grid=(tokens // tm,),
        in_specs=[row_spec, _const_spec((1, d)),
                  pl.BlockSpec((d, N_SPLITS * d), lambda i: (0, 0),
                               pipeline_mode=pl.Buffered(1)),
                  _const_spec((1, d)), _const_spec((1, d)), rope_spec, rope_spec],
        out_specs=[row_spec, row_spec, row_spec, head_rows, head_rows, head_rows,
                   head_rows, row_spec, row_spec],
        out_shape=[act, act, act, head_act, head_act, head_act, head_act, act, act],
        compiler_params=pltpu.CompilerParams(
            dimension_semantics=("arbitrary",), vmem_limit_bytes=VMEM_LIMIT_BYTES),
        name="in_proj",
    )(x2, ln_pre_g, w_in[0].astype(BF16), gm_ln_g, gm_ln_b, cos_t, sin_t)

    hb = ATTN_HEADS_PER_STEP
    head_spec = pl.BlockSpec((1, hb, seq, LANES), lambda b, h: (b, h, 0, 0))
    lam_spec = _const_spec((1, DA_QK_DIM))
    yb = pl.pallas_call(
        _attn_kernel,
        grid=(batch, DA_HEADS // hb),
        in_specs=[lam_spec, lam_spec, lam_spec, lam_spec, _const_spec((1, DA_V_DIM)),
                  head_spec, head_spec, head_spec, head_spec],
        out_specs=head_spec,
        out_shape=head_act,
        scratch_shapes=[
            pltpu.VMEM((hb, DA_V_DIM, seq), BF16),
            pltpu.VMEM((seq, 2 * ATTN_Q_ROWS), F32),
            pltpu.VMEM((seq, 2 * ATTN_Q_ROWS), F32),
            pltpu.VMEM((seq, 2 * ATTN_Q_ROWS), BF16),
            pltpu.VMEM((seq, 2 * ATTN_Q_ROWS), BF16)],
        compiler_params=pltpu.CompilerParams(
            dimension_semantics=("arbitrary", "arbitrary"),
            vmem_limit_bytes=VMEM_LIMIT_BYTES),
        name="diff_attn",
    )(lambda_q1, lambda_k1, lambda_q2, lambda_k2, da_subln_g, q, k, vv, zb)

    to = OUT_ROWS
    out_tiles = seq // to
    orow = pl.BlockSpec((to, d), lambda i: (i, 0))
    yb_rows = pl.BlockSpec((1, DA_HEADS, to, LANES),
                           lambda i: (i // out_tiles, 0, i % out_tiles, 0))
    bs_full = jnp.repeat(gm_bs[0].T, GM_HEAD_DIM, axis=1)
    out = pl.pallas_call(
        _out_kernel,
        grid=(tokens // to,),
        in_specs=[orow, orow, orow, orow, yb_rows, orow, orow,
                  _const_spec((GM_HEADS, CHUNK, CHUNK)), _const_spec((CHUNK, d)),
                  _const_spec((d, d)), _const_spec((d, d)), _const_spec((d, d)),
                  _const_spec((1, d))],
        out_specs=orow,
        out_shape=jax.ShapeDtypeStruct((tokens, d), F32),
        compiler_params=pltpu.CompilerParams(
            dimension_semantics=("arbitrary",), vmem_limit_bytes=VMEM_LIMIT_BYTES),
        name="gate_out",
    )(x2, u, vn, za, yb, ga, gb,
      gm_ws[0].astype(BF16), bs_full, w_branch_a[0].astype(BF16),
      w_branch_b[0].astype(BF16), w_out[0].astype(BF16), ln_post_g)
    return out.reshape(batch, seq, d)
```

```python
import math

import jax
import jax.numpy as jnp
from jax import lax
from jax.experimental import pallas as pl
from jax.experimental.pallas import tpu as pltpu

D_MODEL = 1024
GM_HEADS = 8
GM_HEAD_DIM = 128
CHUNK = 128
DA_HEADS = 8
DA_QK_DIM = 64
DA_V_DIM = 128
ROPE_THETA = 10000.0
EPS = 1e-6
N_SPLITS = 9
LAM_INIT = 0.8 - 0.6 * math.exp(-0.3 * 0)
LOG2_E = math.log2(math.e)

LANES = 128
SUBLANES = 8
VMEM_LIMIT_BYTES = 56 * 1024 * 1024

PROJ_ROWS = 512
LN_ROWS = 128
ATTN_HEADS_PER_STEP = 4
ATTN_Q_ROWS = 256
ATTN_KEY_CHUNK = 256
ATTN_EXP_CHUNK = 64
ATTN_VALUE_SPLITS = 4
OUT_ROWS = 512

F32 = jnp.float32
BF16 = jnp.bfloat16


def _sigmoid(x):
    return 0.5 * jnp.tanh(0.5 * x) + 0.5


def _silu(x):
    h = 0.5 * x
    return h + h * jnp.tanh(h)


def _rope_slab(xs, cos, sin_signed, first_half):
    partner = jnp.where(first_half, pltpu.roll(xs, 96, 1), pltpu.roll(xs, 32, 1))
    return xs * cos + partner * sin_signed


def _proj_kernel(x_ref, g_ref, w_ref, lng_ref, lnb_ref, cos_ref, sin_ref,
                 u_o, vn_o, za_o, q_o, k_o, vv_o, zb_o, ga_o, gb_o):
    def normed(rs):
        x = x_ref[rs, :]
        hx = x * lax.rsqrt(jnp.mean(x * x, axis=-1, keepdims=True) + EPS)
        return (hx * g_ref[...]).astype(BF16)

    def weight(j):
        return w_ref[:, j * D_MODEL:(j + 1) * D_MODEL]

    half = x_ref.shape[0] // 2
    h_halves = []
    for rs in (slice(0, half), slice(half, 2 * half)):
        h_half = normed(rs)
        h_halves.append(h_half)
        za_o[rs, :] = _silu(jnp.dot(h_half, weight(2),
                                    preferred_element_type=F32)).astype(BF16)
    h = jnp.concatenate(h_halves, axis=0)

    def proj(j):
        return jnp.dot(h, weight(j), preferred_element_type=F32)

    def head_slab(hd):
        return slice(hd * LANES, (hd + 1) * LANES)

    ga_o[...] = _sigmoid(proj(7)).astype(BF16)

    v = proj(1)
    for r in range(0, v.shape[0], LN_ROWS):
        vr = v[r:r + LN_ROWS]
        mu = jnp.mean(vr, axis=-1, keepdims=True)
        vc = vr - mu
        var = jnp.mean(vc * vc, axis=-1, keepdims=True)
        vn_o[r:r + LN_ROWS, :] = (vc * lax.rsqrt(var + EPS) * lng_ref[...]
                                  + lnb_ref[...]).astype(BF16)

    gb_o[...] = _sigmoid(proj(8)).astype(BF16)
    zb = _silu(proj(6)).astype(BF16)
    for hd in range(DA_HEADS):
        zb_o[0, hd] = zb[:, head_slab(hd)]
    vv = proj(5).astype(BF16)
    for hd in range(DA_HEADS):
        vv_o[0, hd] = vv[:, head_slab(hd)]

    cos = cos_ref[...]
    sin_signed = sin_ref[...]
    lane = lax.broadcasted_iota(jnp.int32, cos.shape, 1)
    first_half = (lane & 32) == 0
    q_scale = DA_QK_DIM ** -0.5 * LOG2_E
    q = proj(3)
    for hd in range(DA_HEADS):
        q_o[0, hd] = (_rope_slab(q[:, head_slab(hd)], cos, sin_signed, first_half)
                      * q_scale).astype(BF16)
    k = proj(4)
    for hd in range(DA_HEADS):
        k_o[0, hd] = _rope_slab(k[:, head_slab(hd)], cos, sin_signed,
                                first_half).astype(BF16)
    u_o[...] = proj(0).astype(BF16)


def _attn_kernel(lq1_ref, lk1_ref, lq2_ref, lk2_ref, g_ref,
                 q_ref, k_ref, v_ref, zb_ref, o_ref, vt_scr, s0_scr, s1_scr,
                 e0_scr, e1_scr):
    lam = (jnp.exp(jnp.sum(lq1_ref[...] * lk1_ref[...], axis=-1, keepdims=True))
           - jnp.exp(jnp.sum(lq2_ref[...] * lk2_ref[...], axis=-1, keepdims=True))
           + LAM_INIT)
    g = g_ref[...] * (1.0 - LAM_INIT)
    heads = q_ref.shape[1]
    seq = q_ref.shape[2]
    tq = ATTN_Q_ROWS
    tiles_per_head = seq // tq
    n_tiles = heads * tiles_per_head
    for hd in range(heads):
        vt_scr[hd] = v_ref[0, hd].T
    lane = lax.broadcasted_iota(jnp.int32, (tq, LANES), 1)
    is_map1 = lane < DA_QK_DIM

    s_slots = (s0_scr, s1_scr)
    e_slots = (e0_scr, e1_scr)

    def locate(j):
        if isinstance(j, int):
            return j // tiles_per_head, (j % tiles_per_head) * tq
        hd = lax.div(j, tiles_per_head)
        r0 = pl.multiple_of(lax.rem(j, tiles_per_head) * tq, tq)
        return hd, r0

    n_key_chunks = seq // ATTN_KEY_CHUNK
    n_exp_chunks = seq // ATTN_EXP_CHUNK

    def masked_queries(j):
        hd, r0 = locate(j)
        q = q_ref[0, hd, pl.ds(r0, tq), :]
        zero = jnp.zeros_like(q)
        return jnp.concatenate([jnp.where(is_map1, q, zero),
                                jnp.where(is_map1, zero, q)], axis=0)

    def score_chunk(hd, q12, slot, c, m_acc):
        ks = pl.ds(c * ATTN_KEY_CHUNK, ATTN_KEY_CHUNK)
        s_c = lax.dot_general(k_ref[0, hd, ks, :], q12, (((1,), (1,)), ((), ())),
                              preferred_element_type=F32)
        s_slots[slot][ks, :] = s_c
        return jnp.maximum(m_acc, jnp.max(s_c.reshape(-1, SUBLANES, 2 * tq), axis=0))

    def exp_chunk(slot, c, m, l_acc):
        ks = pl.ds(c * ATTN_EXP_CHUNK, ATTN_EXP_CHUNK)
        e = jnp.exp2(s_slots[slot][ks, :] - m)
        e_slots[slot][ks, :] = e.astype(BF16)
        return l_acc + jnp.sum(e.reshape(-1, SUBLANES, 2 * tq), axis=0)

    def value_finish(j, ot, l):
        hd, r0 = locate(j)
        r1 = 1.0 / l[:, :tq]
        r2 = lam / l[:, tq:]
        o = (ot[:, :tq] * r1 - ot[:, tq:] * r2).T
        o = o * lax.rsqrt(jnp.mean(o * o, axis=-1, keepdims=True) + EPS)
        o = o * g * zb_ref[0, hd, pl.ds(r0, tq), :].astype(F32)
        o_ref[0, hd, pl.ds(r0, tq), :] = o.astype(BF16)

    def step(j_score, j_exp, j_value, parity, m_prev, l_prev):
        exp_done = 0
        l_acc = jnp.zeros((SUBLANES, 2 * tq), F32)

        def run_exp(upto, l_acc, exp_done):
            if j_exp is not None:
                for c in range(exp_done, upto):
                    l_acc = exp_chunk(1 - parity, c, m_prev, l_acc)
            return l_acc, upto

        head_share = n_exp_chunks // 4 if j_value is not None else 0
        per_chunk = (n_exp_chunks - head_share) // n_key_chunks
        if j_value is not None:
            hd_v, _ = locate(j_value)
            kv = seq // ATTN_VALUE_SPLITS
            ot = None
            for part in range(ATTN_VALUE_SPLITS):
                ks = pl.ds(part * kv, kv)
                l_acc, exp_done = run_exp(
                    (part + 1) * head_share // ATTN_VALUE_SPLITS, l_acc, exp_done)
                ot_part = jnp.dot(vt_scr[hd_v, :, ks], e_slots[parity][ks, :],
                                  preferred_element_type=F32)
                ot = ot_part if ot is None else ot + ot_part
            value_finish(j_value, ot, l_prev)
        m_new = None
        if j_score is not None:
            hd_s, _ = locate(j_score)
            q12 = masked_queries(j_score)
            m_acc = jnp.full((SUBLANES, 2 * tq), -jnp.inf, F32)
            for c in range(n_key_chunks):
                m_acc = score_chunk(hd_s, q12, parity, c, m_acc)
                l_acc, exp_done = run_exp(
                    min(n_exp_chunks, head_share + (c + 1) * per_chunk),
                    l_acc, exp_done)
            m_new = jnp.max(m_acc, axis=0, keepdims=True)
        l_acc, exp_done = run_exp(n_exp_chunks, l_acc, exp_done)
        l_new = None
        if j_exp is not None:
            l_new = jnp.sum(l_acc, axis=0, keepdims=True)
        return m_new, l_new

    m0, _ = step(0, None, None, 0, None, None)
    m1, l0 = step(1, 0, None, 1, m0, None)

    def body(j, carry):
        return lax.cond(lax.rem(j, 2) == 0,
                        lambda m, l: step(j, j - 1, j - 2, 0, m, l),
                        lambda m, l: step(j, j - 1, j - 2, 1, m, l), *carry)

    m_last, l_prev = lax.fori_loop(2, n_tiles, body, (m1, l0))
    _, l_last = step(None, n_tiles - 1, n_tiles - 2, n_tiles % 2, m_last, l_prev)
    step(None, None, n_tiles - 1, (n_tiles - 1) % 2, None, l_last)


def _out_kernel(x_ref, u_ref, vn_ref, za_ref, yb_ref, ga_ref, gb_ref,
                ws_ref, bs_ref, wa_ref, wb_ref, wo_ref, gpost_ref, o_ref):
    rows = x_ref.shape[0]
    parts = []
    for c in range(rows // CHUNK):
        rs = slice(c * CHUNK, (c + 1) * CHUNK)
        heads = []
        for hd in range(GM_HEADS):
            cs = slice(hd * GM_HEAD_DIM, (hd + 1) * GM_HEAD_DIM)
            heads.append(jnp.dot(ws_ref[hd], vn_ref[rs, cs],
                                 preferred_element_type=F32))
        sv = jnp.concatenate(heads, axis=1) + bs_ref[...]
        ya = u_ref[rs, :].astype(F32) * sv * za_ref[rs, :].astype(F32)
        parts.append(ya.astype(BF16))
    ya = jnp.concatenate(parts, axis=0)
    yb = jnp.concatenate([yb_ref[0, hd] for hd in range(DA_HEADS)], axis=1)
    a = jnp.dot(ya, wa_ref[...], preferred_element_type=F32)
    b = jnp.dot(yb, wb_ref[...], preferred_element_type=F32)
    merged = (ga_ref[...].astype(F32) * a + gb_ref[...].astype(F32) * b).astype(BF16)
    out = jnp.dot(merged, wo_ref[...], preferred_element_type=F32)
    out = out * lax.rsqrt(jnp.mean(out * out, axis=-1, keepdims=True) + EPS)
    o_ref[...] = x_ref[...] + out * gpost_ref[...]


def _rope_tables(seq):
    pos = jnp.arange(seq, dtype=F32)
    inv_freq = 1.0 / (ROPE_THETA ** (jnp.arange(0, DA_QK_DIM, 2, dtype=F32) / DA_QK_DIM))
    ang = pos[:, None] * inv_freq[None, :]
    cos = jnp.cos(ang)
    sin = jnp.sin(ang)
    cos_t = jnp.tile(cos, (1, LANES // (DA_QK_DIM // 2)))
    sin_t = jnp.tile(jnp.concatenate([-sin, sin], axis=1), (1, LANES // DA_QK_DIM))
    return cos_t, sin_t


def _const_spec(shape):
    return pl.BlockSpec(shape, lambda *_: (0,) * len(shape))


def kernel(x, ln_pre_g, w_in, gm_ln_g, gm_ln_b, gm_ws, gm_bs, lambda_q1, lambda_k1,
           lambda_q2, lambda_k2, da_subln_g, w_branch_a, w_branch_b, w_out, ln_post_g):
    batch, seq, d = x.shape
    assert d == D_MODEL and ln_pre_g.shape[0] == 1
    tokens = batch * seq
    x2 = x.reshape(tokens, d)
    cos_t, sin_t = _rope_tables(seq)
    act = jax.ShapeDtypeStruct((tokens, d), BF16)
    head_act = jax.ShapeDtypeStruct((batch, DA_HEADS, seq, LANES), BF16)

    tm = PROJ_ROWS
    seq_tiles = seq // tm
    row_spec = pl.BlockSpec((tm, d), lambda i: (i, 0))
    head_rows = pl.BlockSpec((1, DA_HEADS, tm, LANES),
                             lambda i: (i // seq_tiles, 0, i % seq_tiles, 0))
    rope_spec = pl.BlockSpec((tm, LANES), lambda i: (i % seq_tiles, 0))
    u, vn, za, q, k, vv, zb, ga, gb = pl.pallas_call(
        _proj_kernel,
        grid=(tokens // tm,),
        in_specs=[row_spec, _const_spec((1, d)),
                  pl.BlockSpec((d, N_SPLITS * d), lambda i: (0, 0),
                               pipeline_mode=pl.Buffered(1)),
                  _const_spec((1, d)), _const_spec((1, d)), rope_spec, rope_spec],
        out_specs=[row_spec, row_spec, row_spec, head_rows, head_rows, head_rows,
                   head_rows, row_spec, row_spec],
        out_shape=[act, act, act, head_act, head_act, head_act, head_act, act, act],
        compiler_params=pltpu.CompilerParams(
            dimension_semantics=("arbitrary",), vmem_limit_bytes=VMEM_LIMIT_BYTES),
        name="in_proj",
    )(x2, ln_pre_g, w_in[0].astype(BF16), gm_ln_g, gm_ln_b, cos_t, sin_t)

    hb = ATTN_HEADS_PER_STEP
    head_spec = pl.BlockSpec((1, hb, seq, LANES), lambda b, h: (b, h, 0, 0))
    lam_spec = _const_spec((1, DA_QK_DIM))
    yb = pl.pallas_call(
        _attn_kernel,
        grid=(batch, DA_HEADS // hb),
        in_specs=[lam_spec, lam_spec, lam_spec, lam_spec, _const_spec((1, DA_V_DIM)),
                  head_spec, head_spec, head_spec, head_spec],
        out_specs=head_spec,
        out_shape=head_act,
        scratch_shapes=[
            pltpu.VMEM((hb, DA_V_DIM, seq), BF16),
            pltpu.VMEM((seq, 2 * ATTN_Q_ROWS), F32),
            pltpu.VMEM((seq, 2 * ATTN_Q_ROWS), F32),
            pltpu.VMEM((seq, 2 * ATTN_Q_ROWS), BF16),
            pltpu.VMEM((seq, 2 * ATTN_Q_ROWS), BF16)],
        compiler_params=pltpu.CompilerParams(
            dimension_semantics=("arbitrary", "arbitrary"),
            vmem_limit_bytes=VMEM_LIMIT_BYTES),
        name="diff_attn",
    )(lambda_q1, lambda_k1, lambda_q2, lambda_k2, da_subln_g, q, k, vv, zb)

    to = OUT_ROWS
    out_tiles = seq // to
    orow = pl.BlockSpec((to, d), lambda i: (i, 0))
    yb_rows = pl.BlockSpec((1, DA_HEADS, to, LANES),
                           lambda i: (i // out_tiles, 0, i % out_tiles, 0))
    bs_full = jnp.repeat(gm_bs[0].T, GM_HEAD_DIM, axis=1)
    out = pl.pallas_call(
        _out_kernel,
        grid=(tokens // to,),
        in_specs=[orow, orow, orow, orow, yb_rows, orow, orow,
                  _const_spec((GM_HEADS, CHUNK, CHUNK)), _const_spec((CHUNK, d)),
                  _const_spec((d, d)), _const_spec((d, d)), _const_spec((d, d)),
                  _const_spec((1, d))],
        out_specs=orow,
        out_shape=jax.ShapeDtypeStruct((tokens, d), F32),
        compiler_params=pltpu.CompilerParams(
            dimension_semantics=("arbitrary",), vmem_limit_bytes=VMEM_LIMIT_BYTES),
        name="gate_out",
    )(x2, u, vn, za, yb, ga, gb,
      gm_ws[0].astype(BF16), bs_full, w_branch_a[0].astype(BF16),
      w_branch_b[0].astype(BF16), w_out[0].astype(BF16), ln_post_g)
    return out.reshape(batch, seq, d)
```

```python
import math

import jax
import jax.numpy as jnp
from jax import lax
from jax.experimental import pallas as pl
from jax.experimental.pallas import tpu as pltpu

D_MODEL = 1024
GM_HEADS = 8
GM_HEAD_DIM = 128
CHUNK = 128
DA_HEADS = 8
DA_QK_DIM = 64
DA_V_DIM = 128
ROPE_THETA = 10000.0
EPS = 1e-6
N_SPLITS = 9
LAM_INIT = 0.8 - 0.6 * math.exp(-0.3 * 0)
LOG2_E = math.log2(math.e)

LANES = 128
SUBLANES = 8
VMEM_LIMIT_BYTES = 56 * 1024 * 1024

PROJ_ROWS = 512
LN_ROWS = 128
ATTN_HEADS_PER_STEP = 4
ATTN_Q_ROWS = 256
ATTN_KEY_CHUNK = 256
ATTN_EXP_CHUNK = 64
OUT_ROWS = 512

F32 = jnp.float32
BF16 = jnp.bfloat16


def _sigmoid(x):
    return 0.5 * jnp.tanh(0.5 * x) + 0.5


def _silu(x):
    h = 0.5 * x
    return h + h * jnp.tanh(h)


def _rope_slab(xs, cos, sin_signed, first_half):
    partner = jnp.where(first_half, pltpu.roll(xs, 96, 1), pltpu.roll(xs, 32, 1))
    return xs * cos + partner * sin_signed


def _proj_kernel(x_ref, g_ref, w_ref, lng_ref, lnb_ref, cos_ref, sin_ref,
                 uz_o, vn_o, q_o, k_o, vv_o, zb_o, ga_o, gb_o, za_scr):
    def normed(rs):
        x = x_ref[rs, :]
        hx = x * lax.rsqrt(jnp.mean(x * x, axis=-1, keepdims=True) + EPS)
        return (hx * g_ref[...]).astype(BF16)

    def weight(j):
        return w_ref[:, j * D_MODEL:(j + 1) * D_MODEL]

    half = x_ref.shape[0] // 2
    h_halves = []
    for rs in (slice(0, half), slice(half, 2 * half)):
        h_half = normed(rs)
        h_halves.append(h_half)
        za_scr[rs, :] = _silu(jnp.dot(h_half, weight(2),
                                      preferred_element_type=F32))
    h = jnp.concatenate(h_halves, axis=0)

    def proj(j):
        return jnp.dot(h, weight(j), preferred_element_type=F32)

    def head_slab(hd):
        return slice(hd * LANES, (hd + 1) * LANES)

    ga_o[...] = _sigmoid(proj(7)).astype(BF16)

    v = proj(1)
    for r in range(0, v.shape[0], LN_ROWS):
        vr = v[r:r + LN_ROWS]
        mu = jnp.mean(vr, axis=-1, keepdims=True)
        vc = vr - mu
        var = jnp.mean(vc * vc, axis=-1, keepdims=True)
        vn_o[r:r + LN_ROWS, :] = (vc * lax.rsqrt(var + EPS) * lng_ref[...]
                                  + lnb_ref[...]).astype(BF16)

    gb_o[...] = _sigmoid(proj(8)).astype(BF16)
    zb = _silu(proj(6)).astype(BF16)
    for hd in range(DA_HEADS):
        zb_o[0, hd] = zb[:, head_slab(hd)]
    vv = proj(5).astype(BF16)
    for hd in range(DA_HEADS):
        vv_o[0, hd] = vv[:, head_slab(hd)]

    cos = cos_ref[...]
    sin_signed = sin_ref[...]
    lane = lax.broadcasted_iota(jnp.int32, cos.shape, 1)
    first_half = (lane & 32) == 0
    q_scale = DA_QK_DIM ** -0.5 * LOG2_E
    q = proj(3)
    for hd in range(DA_HEADS):
        q_o[0, hd] = (_rope_slab(q[:, head_slab(hd)], cos, sin_signed, first_half)
                      * q_scale).astype(BF16)
    k = proj(4)
    for hd in range(DA_HEADS):
        k_o[0, hd] = _rope_slab(k[:, head_slab(hd)], cos, sin_signed,
                                first_half).astype(BF16)
    uz_o[...] = (proj(0) * za_scr[...]).astype(BF16)


def _attn_kernel(lq1_ref, lk1_ref, lq2_ref, lk2_ref, g_ref,
                 q_ref, k_ref, v_ref, zb_ref, o_ref, vt_scr, s0_scr, s1_scr,
                 e0_scr, e1_scr):
    lam = (jnp.exp(jnp.sum(lq1_ref[...] * lk1_ref[...], axis=-1, keepdims=True))
           - jnp.exp(jnp.sum(lq2_ref[...] * lk2_ref[...], axis=-1, keepdims=True))
           + LAM_INIT)
    g = g_ref[...] * (1.0 - LAM_INIT)
    heads = q_ref.shape[1]
    seq = q_ref.shape[2]
    tq = ATTN_Q_ROWS
    tiles_per_head = seq // tq
    n_tiles = heads * tiles_per_head
    for hd in range(heads):
        vt_scr[hd] = v_ref[0, hd].T
    lane = lax.broadcasted_iota(jnp.int32, (tq, LANES), 1)
    is_map1 = lane < DA_QK_DIM

    s_slots = (s0_scr, s1_scr)
    e_slots = (e0_scr, e1_scr)

    def locate(j):
        if isinstance(j, int):
            return j // tiles_per_head, (j % tiles_per_head) * tq
        hd = lax.div(j, tiles_per_head)
        r0 = pl.multiple_of(lax.rem(j, tiles_per_head) * tq, tq)
        return hd, r0

    n_key_chunks = seq // ATTN_KEY_CHUNK
    n_exp_chunks = seq // ATTN_EXP_CHUNK

    def masked_queries(j):
        hd, r0 = locate(j)
        q = q_ref[0, hd, pl.ds(r0, tq), :]
        zero = jnp.zeros_like(q)
        return jnp.concatenate([jnp.where(is_map1, q, zero),
                                jnp.where(is_map1, zero, q)], axis=0)

    def score_chunk(hd, q12, slot, c, m_acc):
        ks = pl.ds(c * ATTN_KEY_CHUNK, ATTN_KEY_CHUNK)
        s_c = lax.dot_general(k_ref[0, hd, ks, :], q12, (((1,), (1,)), ((), ())),
                              preferred_element_type=F32)
        s_slots[slot][ks, :] = s_c
        return jnp.maximum(m_acc, jnp.max(s_c.reshape(-1, SUBLANES, 2 * tq), axis=0))

    def exp_chunk(slot, c, m, l_acc):
        ks = pl.ds(c * ATTN_EXP_CHUNK, ATTN_EXP_CHUNK)
        e = jnp.exp2(s_slots[slot][ks, :] - m)
        e_slots[slot][ks, :] = e.astype(BF16)
        return l_acc + jnp.sum(e.reshape(-1, SUBLANES, 2 * tq), axis=0)

    def value_finish(j, ot, l):
        hd, r0 = locate(j)
        r1 = 1.0 / l[:, :tq]
        r2 = lam / l[:, tq:]
        o = (ot[:, :tq] * r1 - ot[:, tq:] * r2).T
        o = o * lax.rsqrt(jnp.mean(o * o, axis=-1, keepdims=True) + EPS)
        o = o * g * zb_ref[0, hd, pl.ds(r0, tq), :].astype(F32)
        o_ref[0, hd, pl.ds(r0, tq), :] = o.astype(BF16)

    def step(j_score, j_exp, j_value, parity, m_prev, l_prev):
        exp_done = 0
        l_acc = jnp.zeros((SUBLANES, 2 * tq), F32)

        def run_exp(upto, l_acc, exp_done):
            if j_exp is not None:
                for c in range(exp_done, upto):
                    l_acc = exp_chunk(1 - parity, c, m_prev, l_acc)
            return l_acc, upto

        head_share = n_exp_chunks // 4 if j_value is not None else 0
        per_chunk = (n_exp_chunks - head_share) // n_key_chunks
        if j_value is not None:
            hd_v, _ = locate(j_value)
            l_acc, exp_done = run_exp(head_share // 2, l_acc, exp_done)
            ot = jnp.dot(vt_scr[hd_v], e_slots[parity][...],
                         preferred_element_type=F32)
            l_acc, exp_done = run_exp(head_share, l_acc, exp_done)
            value_finish(j_value, ot, l_prev)
        m_new = None
        if j_score is not None:
            hd_s, _ = locate(j_score)
            q12 = masked_queries(j_score)
            m_acc = jnp.full((SUBLANES, 2 * tq), -jnp.inf, F32)
            for c in range(n_key_chunks):
                m_acc = score_chunk(hd_s, q12, parity, c, m_acc)
                l_acc, exp_done = run_exp(head_share + (c + 1) * per_chunk,
                                          l_acc, exp_done)
            m_new = jnp.max(m_acc, axis=0, keepdims=True)
        l_acc, exp_done = run_exp(n_exp_chunks, l_acc, exp_done)
        l_new = None
        if j_exp is not None:
            l_new = jnp.sum(l_acc, axis=0, keepdims=True)
        return m_new, l_new

    m0, _ = step(0, None, None, 0, None, None)
    m1, l0 = step(1, 0, None, 1, m0, None)

    def body(j, carry):
        return lax.cond(lax.rem(j, 2) == 0,
                        lambda m, l: step(j, j - 1, j - 2, 0, m, l),
                        lambda m, l: step(j, j - 1, j - 2, 1, m, l), *carry)

    m_last, l_prev = lax.fori_loop(2, n_tiles, body, (m1, l0))
    _, l_last = step(None, n_tiles - 1, n_tiles - 2, n_tiles % 2, m_last, l_prev)
    step(None, None, n_tiles - 1, (n_tiles - 1) % 2, None, l_last)


def _out_kernel(x_ref, uz_ref, vn_ref, yb_ref, ga_ref, gb_ref,
                ws_ref, bs_ref, wa_ref, wb_ref, wo_ref, gpost_ref, o_ref):
    rows = x_ref.shape[0]
    parts = []
    for c in range(rows // CHUNK):
        rs = slice(c * CHUNK, (c + 1) * CHUNK)
        heads = []
        for hd in range(GM_HEADS):
            cs = slice(hd * GM_HEAD_DIM, (hd + 1) * GM_HEAD_DIM)
            heads.append(jnp.dot(ws_ref[hd], vn_ref[rs, cs],
                                 preferred_element_type=F32))
        sv = jnp.concatenate(heads, axis=1) + bs_ref[...]
        ya = uz_ref[rs, :].astype(F32) * sv
        parts.append(ya.astype(BF16))
    ya = jnp.concatenate(parts, axis=0)
    yb = jnp.concatenate([yb_ref[0, hd] for hd in range(DA_HEADS)], axis=1)
    a = jnp.dot(ya, wa_ref[...], preferred_element_type=F32)
    b = jnp.dot(yb, wb_ref[...], preferred_element_type=F32)
    merged = (ga_ref[...].astype(F32) * a + gb_ref[...].astype(F32) * b).astype(BF16)
    out = jnp.dot(merged, wo_ref[...], preferred_element_type=F32)
    out = out * lax.rsqrt(jnp.mean(out * out, axis=-1, keepdims=True) + EPS)
    o_ref[...] = x_ref[...] + out * gpost_ref[...]


def _rope_tables(seq):
    pos = jnp.arange(seq, dtype=F32)
    inv_freq = 1.0 / (ROPE_THETA ** (jnp.arange(0, DA_QK_DIM, 2, dtype=F32) / DA_QK_DIM))
    ang = pos[:, None] * inv_freq[None, :]
    cos = jnp.cos(ang)
    sin = jnp.sin(ang)
    cos_t = jnp.tile(cos, (1, LANES // (DA_QK_DIM // 2)))
    sin_t = jnp.tile(jnp.concatenate([-sin, sin], axis=1), (1, LANES // DA_QK_DIM))
    return cos_t, sin_t


def _const_spec(shape):
    return pl.BlockSpec(shape, lambda *_: (0,) * len(shape))


def kernel(x, ln_pre_g, w_in, gm_ln_g, gm_ln_b, gm_ws, gm_bs, lambda_q1, lambda_k1,
           lambda_q2, lambda_k2, da_subln_g, w_branch_a, w_branch_b, w_out, ln_post_g):
    batch, seq, d = x.shape
    assert d == D_MODEL and ln_pre_g.shape[0] == 1
    tokens = batch * seq
    x2 = x.reshape(tokens, d)
    cos_t, sin_t = _rope_tables(seq)
    act = jax.ShapeDtypeStruct((tokens, d), BF16)
    head_act = jax.ShapeDtypeStruct((batch, DA_HEADS, seq, LANES), BF16)

    tm = PROJ_ROWS
    seq_tiles = seq // tm
    row_spec = pl.BlockSpec((tm, d), lambda i: (i, 0))
    head_rows = pl.BlockSpec((1, DA_HEADS, tm, LANES),
                             lambda i: (i // seq_tiles, 0, i % seq_tiles, 0))
    rope_spec = pl.BlockSpec((tm, LANES), lambda i: (i % seq_tiles, 0))
    uz, vn, q, k, vv, zb, ga, gb = pl.pallas_call(
        _proj_kernel,
        grid=(tokens // tm,),
        in_specs=[row_spec, _const_spec((1, d)),
                  pl.BlockSpec((d, N_SPLITS * d), lambda i: (0, 0),
                               pipeline_mode=pl.Buffered(1)),
                  _const_spec((1, d)), _const_spec((1, d)), rope_spec, rope_spec],
        out_specs=[row_spec, row_spec, head_rows, head_rows, head_rows,
                   head_rows, row_spec, row_spec],
        out_shape=[act, act, head_act, head_act, head_act, head_act, act, act],
        scratch_shapes=[pltpu.VMEM((tm, d), F32)],
        compiler_params=pltpu.CompilerParams(
            dimension_semantics=("arbitrary",), vmem_limit_bytes=VMEM_LIMIT_BYTES),
        name="in_proj",
    )(x2, ln_pre_g, w_in[0].astype(BF16), gm_ln_g, gm_ln_b, cos_t, sin_t)

    hb = ATTN_HEADS_PER_STEP
    head_spec = pl.BlockSpec((1, hb, seq, LANES), lambda b, h: (b, h, 0, 0))
    lam_spec = _const_spec((1, DA_QK_DIM))
    yb = pl.pallas_call(
        _attn_kernel,
        grid=(batch, DA_HEADS // hb),
        in_specs=[lam_spec, lam_spec, lam_spec, lam_spec, _const_spec((1, DA_V_DIM)),
                  head_spec, head_spec, head_spec, head_spec],
        out_specs=head_spec,
        out_shape=head_act,
        scratch_shapes=[
            pltpu.VMEM((hb, DA_V_DIM, seq), BF16),
            pltpu.VMEM((seq, 2 * ATTN_Q_ROWS), F32),
            pltpu.VMEM((seq, 2 * ATTN_Q_ROWS), F32),
            pltpu.VMEM((seq, 2 * ATTN_Q_ROWS), BF16),
            pltpu.VMEM((seq, 2 * ATTN_Q_ROWS), BF16)],
        compiler_params=pltpu.CompilerParams(
            dimension_semantics=("arbitrary", "arbitrary"),
            vmem_limit_bytes=VMEM_LIMIT_BYTES),
        name="diff_attn",
    )(lambda_q1, lambda_k1, lambda_q2, lambda_k2, da_subln_g, q, k, vv, zb)

    to = OUT_ROWS
    out_tiles = seq // to
    orow = pl.BlockSpec((to, d), lambda i: (i, 0))
    yb_rows = pl.BlockSpec((1, DA_HEADS, to, LANES),
                           lambda i: (i // out_tiles, 0, i % out_tiles, 0))
    bs_full = jnp.repeat(gm_bs[0].T, GM_HEAD_DIM, axis=1)
    out = pl.pallas_call(
        _out_kernel,
        grid=(tokens // to,),
        in_specs=[orow, orow, orow, yb_rows, orow, orow,
                  _const_spec((GM_HEADS, CHUNK, CHUNK)), _const_spec((CHUNK, d)),
                  _const_spec((d, d)), _const_spec((d, d)), _const_spec((d, d)),
                  _const_spec((1, d))],
        out_specs=orow,
        out_shape=jax.ShapeDtypeStruct((tokens, d), F32),
        compiler_params=pltpu.CompilerParams(
            dimension_semantics=("arbitrary",), vmem_limit_bytes=VMEM_LIMIT_BYTES),
        name="gate_out",
    )(x2, uz, vn, yb, ga, gb,
      gm_ws[0].astype(BF16), bs_full, w_branch_a[0].astype(BF16),
      w_branch_b[0].astype(BF16), w_out[0].astype(BF16), ln_post_g)
    return out.reshape(batch, seq, d)
```
